```python
import math
import jax, jax.numpy as jnp
from jax import lax
import numpy as np

D_MODEL = 1024
BATCH = 8
SEQ = 4096
DEPTH = 4

A_HEADS = 8
A_KV_HEADS = 2
A_HEAD_DIM = 64
A_GROUP = A_HEADS // A_KV_HEADS
WINDOW = 128
BLOCK = 128
NUM_BUCKETS = 32
MAX_DISTANCE = 128
R_HEADS = 4
R_QK_DIM = 128
R_V_DIM = 256
CHUNK = 128
ROPE_BASE = 10000.0
D_FF = 2816
CONV_WIDTH = 3
EPS = 1e-6
NEG_INF = -1e30

A_Q = A_HEADS * A_HEAD_DIM
A_KV = A_KV_HEADS * A_HEAD_DIM
R_QK = R_HEADS * R_QK_DIM
R_V = R_HEADS * R_V_DIM
IN_SIZES = (A_Q, A_KV, A_KV, R_QK, R_QK, R_V, R_V, D_MODEL, D_MODEL)
D_IN = sum(IN_SIZES)

kernel_name = "hybrid_swa_sink_retention_convffn_trunk"


def rmsnorm(x, g):
    xf = x.astype(jnp.float32)
    y = xf * lax.rsqrt(jnp.mean(xf * xf, axis=-1, keepdims=True) + EPS)
    return (y * g.astype(jnp.float32)).astype(x.dtype)


def t5_band_buckets():
    i = np.arange(BLOCK)[:, None]
    j = np.arange(2 * BLOCK)[None, :]
    dist = BLOCK + i - j
    n = np.maximum(dist, 0)
    max_exact = NUM_BUCKETS // 2
    large = max_exact + (np.log(np.maximum(n, 1) / max_exact) / math.log(MAX_DISTANCE / max_exact)
                         * (NUM_BUCKETS - max_exact)).astype(np.int32)
    large = np.minimum(large, NUM_BUCKETS - 1)
    bucket = np.where(n < max_exact, n, large).astype(np.int32)
    valid = (dist >= 0) & (dist < WINDOW)
    return bucket, valid


def sliding_window_attention(q, k, v, sinks, rel_table):
    b, s, _ = q.shape
    nb = s // BLOCK
    qb = q.reshape(b, nb, BLOCK, A_KV_HEADS, A_GROUP, A_HEAD_DIM)
    kb = k.reshape(b, nb, BLOCK, A_KV_HEADS, A_HEAD_DIM)
    vb = v.reshape(b, nb, BLOCK, A_KV_HEADS, A_HEAD_DIM)
    pad = ((0, 0), (1, 0), (0, 0), (0, 0), (0, 0))
    k_band = jnp.concatenate([jnp.pad(kb[:, :-1], pad), kb], axis=2)
    v_band = jnp.concatenate([jnp.pad(vb[:, :-1], pad), vb], axis=2)
    scores = jnp.einsum('bnqkgd,bnskd->bkgnqs', qb, k_band).astype(jnp.float32) * (A_HEAD_DIM ** -0.5)
    bucket, valid = t5_band_buckets()
    bias = rel_table.astype(jnp.float32)[jnp.asarray(bucket)]
    bias = jnp.transpose(bias, (2, 0, 1)).reshape(A_KV_HEADS, A_GROUP, BLOCK, 2 * BLOCK)
    block_idx = jnp.arange(nb)[:, None, None]
    key_in_cur = (jnp.arange(2 * BLOCK) >= BLOCK)[None, None, :]
    mask = jnp.asarray(valid)[None] & ((block_idx > 0) | key_in_cur)
    scores = jnp.where(mask[None, None, None], scores + bias[None, :, :, None], NEG_INF)
    sink = sinks.astype(jnp.float32).reshape(A_KV_HEADS, A_GROUP)[None, :, :, None, None, None]
    m = jnp.maximum(jnp.max(scores, axis=-1, keepdims=True), sink)
    p = jnp.exp(scores - m)
    probs = p / (jnp.sum(p, axis=-1, keepdims=True) + jnp.exp(sink - m))
    out = jnp.einsum('bkgnqs,bnskd->bnqkgd', probs.astype(v.dtype), v_band)
    return out.reshape(b, s, A_Q)


def rotate(x, pos):
    half = x.shape[-1] // 2
    freqs = ROPE_BASE ** (-jnp.arange(half, dtype=jnp.float32) / half)
    ang = pos.astype(jnp.float32)[:, None] * freqs[None, :]
    cos = jnp.cos(ang)[None, :, None, :]
    sin = jnp.sin(ang)[None, :, None, :]
    x1, x2 = x[..., :half], x[..., half:]
    return jnp.concatenate([x1 * cos - x2 * sin, x1 * sin + x2 * cos], axis=-1).astype(x.dtype)


def retention(q, k, v, g, norm_gain):
    b, s, _ = q.shape
    nc = s // CHUNK
    pos = jnp.arange(s)
    q = rotate(q.reshape(b, s, R_HEADS, R_QK_DIM), pos)
    k = rotate(k.reshape(b, s, R_HEADS, R_QK_DIM), pos) * (R_QK_DIM ** -0.5)
    vh = v.reshape(b, s, R_HEADS, R_V_DIM)
    to_chunks = lambda t: jnp.transpose(t.reshape(b, nc, CHUNK, R_HEADS, t.shape[-1]), (0, 3, 1, 2, 4))
    qc, kc, vc = to_chunks(q), to_chunks(k), to_chunks(vh)
    log_gamma = jnp.log(1.0 - 2.0 ** (-5.0 - jnp.arange(R_HEADS, dtype=jnp.float32)))
    idx = jnp.arange(CHUNK, dtype=jnp.float32)
    rel = idx[:, None] - idx[None, :]
    decay_intra = jnp.where(rel >= 0, jnp.exp(log_gamma[:, None, None] * jnp.maximum(rel, 0.0)), 0.0)
    qk = jnp.einsum('bhncd,bhnsd->bhncs', qc, kc) * decay_intra[None, :, None]
    y_intra = jnp.einsum('bhncs,bhnse->bhnce', qk, vc)
    zeta = jnp.exp(log_gamma[:, None] * (CHUNK - 1 - idx)[None, :])
    kv_chunks = jnp.einsum('bhncd,bhnce->nbhde', kc * zeta[None, :, None, :, None], vc)
    gamma_c = jnp.exp(log_gamma * CHUNK)[None, :, None, None]

    def step(state, kv):
        return gamma_c * state + kv, state

    init = jnp.zeros(kv_chunks.shape[1:], kv_chunks.dtype)
    _, state_prev = lax.scan(step, init, kv_chunks)
    xi = jnp.exp(log_gamma[:, None] * (idx + 1.0)[None, :])
    y_cross = jnp.einsum('bhncd,nbhde->bhnce', qc, state_prev) * xi[None, :, None, :, None]
    y = jnp.transpose(y_intra + y_cross, (0, 2, 3, 1, 4)).reshape(b, s, R_HEADS, R_V_DIM)
    yf = y.astype(jnp.float32)
    mu = jnp.mean(yf, axis=-1, keepdims=True)
    var = jnp.mean((yf - mu) ** 2, axis=-1, keepdims=True)
    yn = ((yf - mu) * lax.rsqrt(var + EPS)).reshape(b, s, R_V) * norm_gain.astype(jnp.float32)
    return (jax.nn.silu(g.astype(jnp.float32)) * yn).astype(v.dtype)


def conv_ffn(h, w_up, conv_w, conv_b, w_down):
    s = h.shape[1]
    u = h @ w_up
    a, val = u[..., :D_FF], u[..., D_FF:]
    a_pad = jnp.pad(a, ((0, 0), (CONV_WIDTH - 1, 0), (0, 0)))
    a_conv = conv_b + sum(conv_w[t] * a_pad[:, t:t + s] for t in range(CONV_WIDTH))
    return (jax.nn.gelu(a_conv) * val) @ w_down


def setup_inputs(seed: int = 0) -> dict:
    key = jax.random.key(seed)
    ks = jax.random.split(key, 20)
    nrm = lambda k, shape, scale: jax.random.normal(k, shape, jnp.float32) * scale
    gain = lambda k, shape: 1.0 + 0.05 * jax.random.normal(k, shape, jnp.float32)
    return {
        "x": nrm(ks[0], (BATCH, SEQ, D_MODEL), 1.0),
        "norm_pre_mix": gain(ks[1], (DEPTH, D_MODEL)),
        "w_in": nrm(ks[2], (DEPTH, D_MODEL, D_IN), D_MODEL ** -0.5),
        "sinks": nrm(ks[3], (DEPTH, A_HEADS), 0.5),
        "rel_bias": nrm(ks[4], (NUM_BUCKETS, A_HEADS), 0.5),
        "ret_norm": gain(ks[5], (DEPTH, R_V)),
        "w_out_a": nrm(ks[6], (DEPTH, A_Q, D_MODEL), A_Q ** -0.5),
        "w_out_r": nrm(ks[7], (DEPTH, R_V, D_MODEL), R_V ** -0.5),
        "w_out": nrm(ks[8], (DEPTH, D_MODEL, D_MODEL), D_MODEL ** -0.5),
        "norm_post_mix": gain(ks[9], (DEPTH, D_MODEL)),
        "norm_pre_ffn": gain(ks[10], (DEPTH, D_MODEL)),
        "w_up": nrm(ks[11], (DEPTH, D_MODEL, 2 * D_FF), D_MODEL ** -0.5),
        "conv_w": nrm(ks[12], (DEPTH, CONV_WIDTH, D_FF), CONV_WIDTH ** -0.5),
        "conv_b": nrm(ks[13], (DEPTH, D_FF), 0.02),
        "w_down": nrm(ks[14], (DEPTH, D_FF, D_MODEL), D_FF ** -0.5),
        "norm_post_ffn": gain(ks[15], (DEPTH, D_MODEL)),
    }


def reference(x, norm_pre_mix, w_in, sinks, rel_bias, ret_norm, w_out_a, w_out_r, w_out,
              norm_post_mix, norm_pre_ffn, w_up, conv_w, conv_b, w_down, norm_post_ffn):
    split_points = list(np.cumsum(IN_SIZES)[:-1])
    for l in range(DEPTH):
        h = rmsnorm(x, norm_pre_mix[l])
        proj = h @ w_in[l]
        qa, ka, va, qr, kr, vr, gr, gate_a, gate_r = jnp.split(proj, split_points, axis=-1)
        y_a = sliding_window_attention(qa, ka, va, sinks[l], rel_bias) @ w_out_a[l]
        y_r = retention(qr, kr, vr, gr, ret_norm[l]) @ w_out_r[l]
        merged = jax.nn.sigmoid(gate_a) * y_a + jax.nn.sigmoid(gate_r) * y_r
        x = x + rmsnorm(merged @ w_out[l], norm_post_mix[l])
        h = rmsnorm(x, norm_pre_ffn[l])
        x = x + rmsnorm(conv_ffn(h, w_up[l], conv_w[l], conv_b[l], w_down[l]), norm_post_ffn[l])
    return x
```

```python
import functools
import math

import numpy as np
import jax
import jax.numpy as jnp
from jax import lax
from jax.experimental import pallas as pl
from jax.experimental.pallas import tpu as pltpu

D_MODEL = 1024
A_HEADS = 8
A_KV_HEADS = 2
A_HEAD_DIM = 64
A_GROUP = A_HEADS // A_KV_HEADS
WINDOW = 128
BLOCK = 128
NUM_BUCKETS = 32
MAX_DISTANCE = 128
R_HEADS = 4
R_QK_DIM = 128
R_V_DIM = 256
CHUNK = 128
ROPE_BASE = 10000.0
D_FF = 2816
CONV_WIDTH = 3
EPS = 1e-6
NEG_INF = -1e30

A_Q = A_HEADS * A_HEAD_DIM
A_KV = A_KV_HEADS * A_HEAD_DIM
R_QK = R_HEADS * R_QK_DIM
R_V = R_HEADS * R_V_DIM
OFF_AQ = 0
OFF_AKV = OFF_AQ + A_Q
OFF_RQK = OFF_AKV + 2 * A_KV
OFF_RV = OFF_RQK + 2 * R_QK
OFF_RG = OFF_RV + R_V
OFF_GATES = OFF_RG + R_V
D_IN = OFF_GATES + 2 * D_MODEL

V7X_VMEM_LIMIT_BYTES = 56 * 1024 * 1024
SUBLANES = 8
FF_CHUNK = 256
TM_INPROJ = 512
TM_TAIL = 256

BF16 = jnp.bfloat16
F32 = jnp.float32


def _resident(shape):
    nd = len(shape)
    return pl.BlockSpec(shape, lambda *_: (0,) * nd, pipeline_mode=pl.Buffered(1))


def _rms(v, gain):
    return v * lax.rsqrt(jnp.mean(v * v, axis=-1, keepdims=True) + EPS) * gain


def _t5_band_buckets():
    i = np.arange(BLOCK)[:, None]
    j = np.arange(2 * BLOCK)[None, :]
    dist = BLOCK + i - j
    n = np.maximum(dist, 0)
    max_exact = NUM_BUCKETS // 2
    large = max_exact + (np.log(np.maximum(n, 1) / max_exact) / math.log(MAX_DISTANCE / max_exact)
                         * (NUM_BUCKETS - max_exact)).astype(np.int32)
    large = np.minimum(large, NUM_BUCKETS - 1)
    bucket = np.where(n < max_exact, n, large).astype(np.int32)
    valid = (dist >= 0) & (dist < WINDOW)
    return bucket, valid


def _rotary_tables(seq):
    half = R_QK_DIM // 2
    freqs = ROPE_BASE ** (-jnp.arange(half, dtype=F32) / half)
    ang = jnp.arange(seq).astype(F32)[:, None] * freqs[None, :]
    cos, sin = jnp.cos(ang), jnp.sin(ang)
    return jnp.concatenate([cos, cos], axis=-1), jnp.concatenate([-sin, sin], axis=-1)


def _retention_tables():
    log_gamma = jnp.log(1.0 - 2.0 ** (-5.0 - jnp.arange(R_HEADS, dtype=F32)))
    idx = jnp.arange(CHUNK, dtype=F32)
    rel = idx[:, None] - idx[None, :]
    decay = jnp.where(rel >= 0, jnp.exp(log_gamma[:, None, None] * jnp.maximum(rel, 0.0)), 0.0)
    zeta = jnp.exp(log_gamma[:, None] * (CHUNK - 1 - idx)[None, :])
    xi = jnp.exp(log_gamma[:, None] * (idx + 1.0)[None, :])
    gamma_c = jnp.exp(log_gamma * CHUNK)
    zeta_b = jnp.broadcast_to(zeta[:, :, None], (R_HEADS, CHUNK, R_QK_DIM))
    xi_b = jnp.broadcast_to(xi[:, :, None], (R_HEADS, CHUNK, R_V_DIM))
    return decay, zeta_b, xi_b, gamma_c


def _bias_kernel(tab_ref, bucket_ref, valid_ref, out_ref):
    bucket = bucket_ref[...]
    valid = valid_ref[...] > 0
    in_cur = lax.broadcasted_iota(jnp.int32, (BLOCK, 2 * BLOCK), 1) >= BLOCK
    for h in range(A_HEADS):
        acc = jnp.zeros((BLOCK, 2 * BLOCK), F32)
        for b in range(NUM_BUCKETS):
            acc = jnp.where(bucket == b, tab_ref[b, h], acc)
        out_ref[1, h] = jnp.where(valid, acc, NEG_INF)
        out_ref[0, h] = jnp.where(valid & in_cur, acc, NEG_INF)


def _bias_table(rel_bias):
    bucket, valid = _t5_band_buckets()
    return pl.pallas_call(
        _bias_kernel,
        out_shape=jax.ShapeDtypeStruct((2, A_HEADS, BLOCK, 2 * BLOCK), F32),
        in_specs=[pl.BlockSpec(memory_space=pltpu.SMEM),
                  pl.BlockSpec(memory_space=pltpu.VMEM),
                  pl.BlockSpec(memory_space=pltpu.VMEM)],
        out_specs=pl.BlockSpec(memory_space=pltpu.VMEM),
        name="bias_table",
    )(rel_bias.astype(F32), jnp.asarray(bucket), jnp.asarray(valid.astype(np.int32)))


def _inproj_kernel(x_ref, g_ref, w_ref, cos_ref, sin_ref,
                   qa_ref, kva_ref, qkr_ref, vr_ref, gr_ref, gates_ref):
    h = _rms(x_ref[...], g_ref[...]).astype(BF16)

    def proj(lo, hi):
        return jnp.dot(h, w_ref[:, lo:hi], preferred_element_type=F32)

    qa_ref[...] = (proj(OFF_AQ, OFF_AKV) * (A_HEAD_DIM ** -0.5)).astype(BF16)
    kva_ref[...] = proj(OFF_AKV, OFF_RQK).astype(BF16)
    qk = proj(OFF_RQK, OFF_RV)
    cos = cos_ref[...]
    sin = sin_ref[...]
    for j in range(2 * R_HEADS):
        xh = qk[:, j * R_QK_DIM:(j + 1) * R_QK_DIM]
        rot = xh * cos + pltpu.roll(xh, R_QK_DIM // 2, 1) * sin
        if j >= R_HEADS:
            rot = rot * (R_QK_DIM ** -0.5)
        qkr_ref[:, j * R_QK_DIM:(j + 1) * R_QK_DIM] = rot.astype(BF16)
    vr_ref[...] = proj(OFF_RV, OFF_RG).astype(BF16)
    gr_ref[...] = proj(OFF_RG, OFF_GATES).astype(BF16)
    gates_ref[...] = proj(OFF_GATES, D_IN).astype(BF16)


def _inproj(xf, gain, w, cos, sin, seq):
    t = xf.shape[0]
    tm = min(TM_INPROJ, seq)
    tiles_per_seq = seq // tm
    row = lambda n: pl.BlockSpec((tm, n), lambda i: (i, 0))
    pos = pl.BlockSpec((tm, R_QK_DIM), lambda i: (i % tiles_per_seq, 0))
    widths = (A_Q, 2 * A_KV, 2 * R_QK, R_V, R_V, 2 * D_MODEL)
    return pl.pallas_call(
        _inproj_kernel,
        grid=(t // tm,),
        in_specs=[row(D_MODEL), _resident((1, D_MODEL)), _resident((D_MODEL, D_IN)), pos, pos],
        out_specs=[row(n) for n in widths],
        out_shape=[jax.ShapeDtypeStruct((t, n), BF16) for n in widths],
        compiler_params=pltpu.CompilerParams(dimension_semantics=("parallel",),
                                             vmem_limit_bytes=V7X_VMEM_LIMIT_BYTES),
        name="inproj",
    )(xf, gain.reshape(1, D_MODEL), w, cos, sin)


def _mixer_kernel(sink_ref, gam_ref, qa_ref, kvc_ref, kvp_ref, qkr_ref, vr_ref, gr_ref,
                  bias_ref, dec_ref, zeta_ref, xi_ref, rn_ref, ao_ref, ro_ref, state_ref):
    @pl.when(pl.program_id(1) == 0)
    def _():
        state_ref[...] = jnp.zeros_like(state_ref)

    contract_last = (((1,), (1,)), ((), ()))
    contract_first = (((0,), (0,)), ((), ()))

    outs = []
    for kvh in range(A_KV_HEADS):
        ks = slice(kvh * A_HEAD_DIM, (kvh + 1) * A_HEAD_DIM)
        vs = slice(A_KV + kvh * A_HEAD_DIM, A_KV + (kvh + 1) * A_HEAD_DIM)
        k_band = jnp.concatenate([kvp_ref[:, ks], kvc_ref[:, ks]], axis=0)
        v_band = jnp.concatenate([kvp_ref[:, vs], kvc_ref[:, vs]], axis=0)
        for g in range(A_GROUP):
            hd = kvh * A_GROUP + g
            q = qa_ref[:, hd * A_HEAD_DIM:(hd + 1) * A_HEAD_DIM]
            s = lax.dot_general(q, k_band, contract_last, preferred_element_type=F32) + bias_ref[0, hd]
            sink = sink_ref[hd]
            m = jnp.maximum(jnp.max(s, axis=-1, keepdims=True), sink)
            p = jnp.exp(s - m)
            denom = jnp.sum(p, axis=-1, keepdims=True) + jnp.exp(sink - m)
            probs = (p / denom).astype(BF16)
            outs.append(jnp.dot(probs, v_band, preferred_element_type=F32))
    ao_ref[...] = jnp.concatenate(outs, axis=-1).astype(BF16)

    for hd in range(R_HEADS):
        q = qkr_ref[:, hd * R_QK_DIM:(hd + 1) * R_QK_DIM]
        k = qkr_ref[:, R_QK + hd * R_QK_DIM:R_QK + (hd + 1) * R_QK_DIM]
        vsl = slice(hd * R_V_DIM, (hd + 1) * R_V_DIM)
        v = vr_ref[:, vsl]
        st = state_ref[hd]
        qk = lax.dot_general(q, k, contract_last, preferred_element_type=F32) * dec_ref[hd]
        y = jnp.dot(qk.astype(BF16), v, preferred_element_type=F32)
        y = y + jnp.dot(q, st.astype(BF16), preferred_element_type=F32) * xi_ref[hd]
        kz = (k.astype(F32) * zeta_ref[hd]).astype(BF16)
        kv = lax.dot_general(kz, v, contract_first, preferred_element_type=F32)
        state_ref[hd] = gam_ref[hd] * st + kv
        mu = jnp.mean(y, axis=-1, keepdims=True)
        d = y - mu
        var = jnp.mean(d * d, axis=-1, keepdims=True)
        yn = d * lax.rsqrt(var + EPS) * rn_ref[:, vsl]
        gate = gr_ref[:, vsl].astype(F32)
        ro_ref[:, vsl] = (gate * jax.nn.sigmoid(gate) * yn).astype(BF16)


def _mixer(qa, kva, qkr, vr, gr, bias, sinks, ret_norm, tables, batch, seq):
    decay, zeta_b, xi_b, gamma_c = tables
    t = qa.shape[0]
    nc = seq // CHUNK
    cur = lambda n: pl.BlockSpec((CHUNK, n), lambda b, c: (b * nc + c, 0))
    prev = pl.BlockSpec((CHUNK, 2 * A_KV), lambda b, c: (b * nc + jnp.maximum(c - 1, 0), 0))
    bias_spec = pl.BlockSpec((1, A_HEADS, BLOCK, 2 * BLOCK), lambda b, c: (jnp.minimum(c, 1), 0, 0, 0))
    smem = pl.BlockSpec(memory_space=pltpu.SMEM)
    return pl.pallas_call(
        _mixer_kernel,
        grid=(batch, nc),
        in_specs=[smem, smem, cur(A_Q), cur(2 * A_KV), prev, cur(2 * R_QK), cur(R_V), cur(R_V), bias_spec,
                  _resident((R_HEADS, CHUNK, CHUNK)), _resident((R_HEADS, CHUNK, R_QK_DIM)),
                  _resident((R_HEADS, CHUNK, R_V_DIM)), _resident((1, R_V))],
        out_specs=[cur(A_Q), cur(R_V)],
        scratch_shapes=[pltpu.VMEM((R_HEADS, R_QK_DIM, R_V_DIM), F32)],
        out_shape=[jax.ShapeDtypeStruct((t, A_Q), BF16), jax.ShapeDtypeStruct((t, R_V), BF16)],
        compiler_params=pltpu.CompilerParams(dimension_semantics=("parallel", "arbitrary"),
                                             vmem_limit_bytes=V7X_VMEM_LIMIT_BYTES),
        name="mixer",
    )(sinks.astype(F32), gamma_c, qa, kva, kva, qkr, vr, gr, bias, decay, zeta_b, xi_b,
      ret_norm.reshape(1, R_V))


def _gelu_tanh(v):
    return 0.5 * v * (1.0 + jnp.tanh(math.sqrt(2.0 / math.pi) * (v + 0.044715 * (v * v * v))))


def _tail_kernel(tiles_per_seq, x_ref, ao_ref, ro_ref, gates_ref, woa_ref, wor_ref, wo_ref,
                 npm_ref, npf_ref, wup_ref, cw_ref, cb_ref, wdn_ref, nff_ref,
                 out_ref, carry_ref, hid_ref):
    tm = x_ref.shape[0]

    @pl.when(pl.program_id(0) % tiles_per_seq == 0)
    def _():
        carry_ref[...] = jnp.zeros_like(carry_ref)

    ya = jnp.dot(ao_ref[...], woa_ref[...], preferred_element_type=F32)
    yr = jnp.dot(ro_ref[...], wor_ref[...], preferred_element_type=F32)
    gate_a = jax.nn.sigmoid(gates_ref[:, :D_MODEL].astype(F32))
    gate_r = jax.nn.sigmoid(gates_ref[:, D_MODEL:].astype(F32))
    merged = (gate_a * ya + gate_r * yr).astype(BF16)
    x1 = x_ref[...] + _rms(jnp.dot(merged, wo_ref[...], preferred_element_type=F32), npm_ref[...])
    h = _rms(x1, npf_ref[...]).astype(BF16)

    row = lax.broadcasted_iota(jnp.int32, (tm, FF_CHUNK), 0)
    for c in range(D_FF // FF_CHUNK):
        cs = slice(c * FF_CHUNK, (c + 1) * FF_CHUNK)
        a = jnp.dot(h, wup_ref[:, cs], preferred_element_type=F32)
        val = jnp.dot(h, wup_ref[:, D_FF + c * FF_CHUNK:D_FF + (c + 1) * FF_CHUNK], preferred_element_type=F32)
        last = carry_ref[SUBLANES - 1:SUBLANES, cs]
        last2 = carry_ref[SUBLANES - 2:SUBLANES - 1, cs]
        a1 = jnp.where(row == 0, last, pltpu.roll(a, 1, 0))
        a2 = jnp.where(row == 0, last2, jnp.where(row == 1, last, pltpu.roll(a, 2, 0)))
        conv = cb_ref[:, cs] + cw_ref[0:1, cs] * a2 + cw_ref[1:2, cs] * a1 + cw_ref[2:3, cs] * a
        carry_ref[:, cs] = a[tm - SUBLANES:, :]
        hid_ref[:, cs] = (_gelu_tanh(conv) * val).astype(BF16)

    y = jnp.dot(hid_ref[...], wdn_ref[...], preferred_element_type=F32)
    out_ref[...] = x1 + _rms(y, nff_ref[...])


def _tail(xf, ao, ro, gates, woa, wor, wo, npm, npf, wup, cw, cb, wdn, nff, seq):
    t = xf.shape[0]
    tm = min(TM_TAIL, seq)
    row = lambda n: pl.BlockSpec((tm, n), lambda i: (i, 0))
    vec = lambda v: v.reshape(1, -1).astype(F32)
    return pl.pallas_call(
        functools.partial(_tail_kernel, seq // tm),
        grid=(t // tm,),
        in_specs=[row(D_MODEL), row(A_Q), row(R_V), row(2 * D_MODEL),
                  _resident((A_Q, D_MODEL)), _resident((R_V, D_MODEL)), _resident((D_MODEL, D_MODEL)),
                  _resident((1, D_MODEL)), _resident((1, D_MODEL)),
                  _resident((D_MODEL, 2 * D_FF)), _resident((CONV_WIDTH, D_FF)), _resident((1, D_FF)),
                  _resident((D_FF, D_MODEL)), _resident((1, D_MODEL))],
        out_specs=row(D_MODEL),
        out_shape=jax.ShapeDtypeStruct((t, D_MODEL), F32),
        scratch_shapes=[pltpu.VMEM((SUBLANES, D_FF), F32), pltpu.VMEM((tm, D_FF), BF16)],
        compiler_params=pltpu.CompilerParams(dimension_semantics=("arbitrary",),
                                             vmem_limit_bytes=V7X_VMEM_LIMIT_BYTES),
        name="tail",
    )(xf, ao, ro, gates, woa, wor, wo, vec(npm), vec(npf), wup, cw.astype(F32), vec(cb), wdn, vec(nff))


def kernel(x, norm_pre_mix, w_in, sinks, rel_bias, ret_norm, w_out_a, w_out_r, w_out, norm_post_mix,
           norm_pre_ffn, w_up, conv_w, conv_b, w_down, norm_post_ffn):
    batch, seq, _ = x.shape
    depth = w_in.shape[0]
    assert seq % CHUNK == 0 and seq % min(TM_INPROJ, seq) == 0 and seq % min(TM_TAIL, seq) == 0
    xf = x.reshape(batch * seq, D_MODEL).astype(F32)
    cos, sin = _rotary_tables(seq)
    tables = _retention_tables()
    bias = _bias_table(rel_bias)
    w_in, w_out_a, w_out_r, w_out, w_up, w_down = (
        w.astype(BF16) for w in (w_in, w_out_a, w_out_r, w_out, w_up, w_down))
    for l in range(depth):
        qa, kva, qkr, vr, gr, gates = _inproj(xf, norm_pre_mix[l].astype(F32), w_in[l], cos, sin, seq)
        ao, ro = _mixer(qa, kva, qkr, vr, gr, bias, sinks[l], ret_norm[l].astype(F32), tables, batch, seq)
        xf = _tail(xf, ao, ro, gates, w_out_a[l], w_out_r[l], w_out[l], norm_post_mix[l], norm_pre_ffn[l],
                   w_up[l], conv_w[l], conv_b[l], w_down[l], norm_post_ffn[l], seq)
    return xf.reshape(batch, seq, D_MODEL).astype(x.dtype)
```

```python
import functools
import math

import numpy as np
import jax
import jax.numpy as jnp
from jax import lax
from jax.experimental import pallas as pl
from jax.experimental.pallas import tpu as pltpu

D_MODEL = 1024
A_HEADS = 8
A_KV_HEADS = 2
A_HEAD_DIM = 64
A_GROUP = A_HEADS // A_KV_HEADS
WINDOW = 128
BLOCK = 128
NUM_BUCKETS = 32
MAX_DISTANCE = 128
R_HEADS = 4
R_QK_DIM = 128
R_V_DIM = 256
CHUNK = 128
ROPE_BASE = 10000.0
D_FF = 2816
CONV_WIDTH = 3
EPS = 1e-6
NEG_INF = -1e30

A_Q = A_HEADS * A_HEAD_DIM
A_KV = A_KV_HEADS * A_HEAD_DIM
R_QK = R_HEADS * R_QK_DIM
R_V = R_HEADS * R_V_DIM
OFF_AQ = 0
OFF_AKV = OFF_AQ + A_Q
OFF_RQK = OFF_AKV + 2 * A_KV
OFF_RV = OFF_RQK + 2 * R_QK
OFF_RG = OFF_RV + R_V
OFF_GATES = OFF_RG + R_V
D_IN = OFF_GATES + 2 * D_MODEL

V7X_VMEM_LIMIT_BYTES = 56 * 1024 * 1024
SUBLANES = 8
LANES = 128
FF_CHUNK = 256
TM_INPROJ = 512
TM_TAIL = 512
MIX_CHUNKS = 2

BF16 = jnp.bfloat16
F32 = jnp.float32


def _resident(shape):
    nd = len(shape)
    return pl.BlockSpec(shape, lambda *_: (0,) * nd, pipeline_mode=pl.Buffered(1))


def _rms(v, gain):
    return v * lax.rsqrt(jnp.mean(v * v, axis=-1, keepdims=True) + EPS) * gain


def _t5_band_buckets():
    i = np.arange(BLOCK)[:, None]
    j = np.arange(2 * BLOCK)[None, :]
    dist = BLOCK + i - j
    n = np.maximum(dist, 0)
    max_exact = NUM_BUCKETS // 2
    large = max_exact + (np.log(np.maximum(n, 1) / max_exact) / math.log(MAX_DISTANCE / max_exact)
                         * (NUM_BUCKETS - max_exact)).astype(np.int32)
    large = np.minimum(large, NUM_BUCKETS - 1)
    bucket = np.where(n < max_exact, n, large).astype(np.int32)
    valid = (dist >= 0) & (dist < WINDOW)
    return bucket, valid


def _rotary_tables(seq):
    half = R_QK_DIM // 2
    freqs = ROPE_BASE ** (-jnp.arange(half, dtype=F32) / half)
    ang = jnp.arange(seq).astype(F32)[:, None] * freqs[None, :]
    cos, sin = jnp.cos(ang), jnp.sin(ang)
    return jnp.concatenate([cos, cos], axis=-1), jnp.concatenate([-sin, sin], axis=-1)


def _retention_tables():
    log_gamma = jnp.log(1.0 - 2.0 ** (-5.0 - jnp.arange(R_HEADS, dtype=F32)))
    idx = jnp.arange(CHUNK, dtype=F32)
    rel = idx[:, None] - idx[None, :]
    decay = jnp.where(rel >= 0, jnp.exp(log_gamma[:, None, None] * jnp.maximum(rel, 0.0)), 0.0)
    zeta = jnp.exp(log_gamma[:, None] * (CHUNK - 1 - idx)[None, :])
    xi = jnp.exp(log_gamma[:, None] * (idx + 1.0)[None, :])
    gamma_c = jnp.exp(log_gamma * CHUNK)
    zeta_b = jnp.broadcast_to(zeta[:, :, None], (R_HEADS, CHUNK, R_QK_DIM))
    xi_b = jnp.broadcast_to(xi[:, :, None], (R_HEADS, CHUNK, R_QK_DIM))
    return decay, zeta_b, xi_b, gamma_c


def _bias_kernel(tab_ref, bucket_ref, valid_ref, out_ref):
    bucket = bucket_ref[...]
    valid = valid_ref[...] > 0
    in_cur = lax.broadcasted_iota(jnp.int32, (BLOCK, 2 * BLOCK), 1) >= BLOCK
    for h in range(A_HEADS):
        acc = jnp.zeros((BLOCK, 2 * BLOCK), F32)
        for b in range(NUM_BUCKETS):
            acc = jnp.where(bucket == b, tab_ref[b, h], acc)
        out_ref[1, h] = jnp.where(valid, acc, NEG_INF)
        out_ref[0, h] = jnp.where(valid & in_cur, acc, NEG_INF)


def _bias_table(rel_bias):
    bucket, valid = _t5_band_buckets()
    return pl.pallas_call(
        _bias_kernel,
        out_shape=jax.ShapeDtypeStruct((2, A_HEADS, BLOCK, 2 * BLOCK), F32),
        in_specs=[pl.BlockSpec(memory_space=pltpu.SMEM),
                  pl.BlockSpec(memory_space=pltpu.VMEM),
                  pl.BlockSpec(memory_space=pltpu.VMEM)],
        out_specs=pl.BlockSpec(memory_space=pltpu.VMEM),
        name="bias_table",
    )(rel_bias.astype(F32), jnp.asarray(bucket), jnp.asarray(valid.astype(np.int32)))


def _inproj_kernel(x_ref, g_ref, w_ref, cos_ref, sin_ref,
                   qa_ref, ka_ref, vext_ref, qkr_ref, vr_ref, gr_ref, gates_ref):
    h = _rms(x_ref[...], g_ref[...]).astype(BF16)

    def proj(lo, hi):
        return jnp.dot(h, w_ref[:, lo:hi], preferred_element_type=F32)

    qa_ref[...] = (proj(OFF_AQ, OFF_AKV) * (A_HEAD_DIM ** -0.5)).astype(BF16)
    kva = proj(OFF_AKV, OFF_RQK)
    ka_ref[...] = kva[:, :A_KV].astype(BF16)
    vv = kva[:, A_KV:]
    low = lax.broadcasted_iota(jnp.int32, vv.shape, 1) < A_HEAD_DIM
    vext_ref[:, :LANES] = jnp.where(low, vv, 1.0).astype(BF16)
    vext_ref[:, LANES:] = jnp.where(low, pltpu.roll(vv, A_HEAD_DIM, 1), 1.0).astype(BF16)
    qk = proj(OFF_RQK, OFF_RV)
    cos = cos_ref[...]
    sin = sin_ref[...]
    for j in range(2 * R_HEADS):
        xh = qk[:, j * R_QK_DIM:(j + 1) * R_QK_DIM]
        rot = xh * cos + pltpu.roll(xh, R_QK_DIM // 2, 1) * sin
        if j >= R_HEADS:
            rot = rot * (R_QK_DIM ** -0.5)
        qkr_ref[:, j * R_QK_DIM:(j + 1) * R_QK_DIM] = rot.astype(BF16)
    vr_ref[...] = proj(OFF_RV, OFF_RG).astype(BF16)
    gr_ref[...] = proj(OFF_RG, OFF_GATES).astype(BF16)
    gates_ref[...] = proj(OFF_GATES, D_IN).astype(BF16)


def _inproj(xf, gain, w, cos, sin, seq):
    t = xf.shape[0]
    tm = min(TM_INPROJ, seq)
    tiles_per_seq = seq // tm
    row = lambda n: pl.BlockSpec((tm, n), lambda i: (i, 0))
    pos = pl.BlockSpec((tm, R_QK_DIM), lambda i: (i % tiles_per_seq, 0))
    widths = (A_Q, A_KV, 2 * LANES, 2 * R_QK, R_V, R_V, 2 * D_MODEL)
    return pl.pallas_call(
        _inproj_kernel,
        grid=(t // tm,),
        in_specs=[row(D_MODEL), _resident((1, D_MODEL)), _resident((D_MODEL, D_IN)), pos, pos],
        out_specs=[row(n) for n in widths],
        out_shape=[jax.ShapeDtypeStruct((t, n), BF16) for n in widths],
        compiler_params=pltpu.CompilerParams(dimension_semantics=("parallel",),
                                             vmem_limit_bytes=V7X_VMEM_LIMIT_BYTES),
        name="inproj",
    )(xf, gain.reshape(1, D_MODEL), w, cos, sin)


def _mixer_kernel(nch, sink_ref, gam_ref, qa_ref, kc_ref, kp_ref, vc_ref, vp_ref, qkr_ref, vr_ref, gr_ref,
                  bias_ref, dec_ref, zeta_ref, xi_ref, rn_ref, ao_ref, ro_ref, state_ref):
    step = pl.program_id(1)

    @pl.when(step == 0)
    def _():
        state_ref[...] = jnp.zeros_like(state_ref)

    contract_last = (((1,), (1,)), ((), ()))
    contract_first = (((0,), (0,)), ((), ()))

    row_head = lax.broadcasted_iota(jnp.int32, (A_GROUP * BLOCK, 1), 0) // BLOCK
    sink_cols = []
    for kvh in range(A_KV_HEADS):
        col = jnp.full((A_GROUP * BLOCK, 1), sink_ref[kvh * A_GROUP], F32)
        for g in range(1, A_GROUP):
            col = jnp.where(row_head == g, sink_ref[kvh * A_GROUP + g], col)
        sink_cols.append(col)
    low = lax.broadcasted_iota(jnp.int32, (BLOCK, LANES), 1) < A_HEAD_DIM

    states = [state_ref[hd] for hd in range(R_HEADS)]
    for c in range(nch):
        rows = slice(c * BLOCK, (c + 1) * BLOCK)
        prows = slice((c - 1) * BLOCK, c * BLOCK)
        variant = jnp.minimum(step, 1) if c == 0 else 1

        pieces = []
        for kvh in range(A_KV_HEADS):
            ks = slice(kvh * A_HEAD_DIM, (kvh + 1) * A_HEAD_DIM)
            vs = slice(kvh * LANES, (kvh + 1) * LANES)
            k_prev = kp_ref[:, ks] if c == 0 else kc_ref[prows, ks]
            v_prev = vp_ref[:, vs] if c == 0 else vc_ref[prows, vs]
            k_band = jnp.concatenate([k_prev, kc_ref[rows, ks]], axis=0)
            v_band = jnp.concatenate([v_prev, vc_ref[rows, vs]], axis=0)
            q = jnp.concatenate([qa_ref[rows, (kvh * A_GROUP + g) * A_HEAD_DIM:(kvh * A_GROUP + g + 1) * A_HEAD_DIM]
                                 for g in range(A_GROUP)], axis=0)
            s = lax.dot_general(q, k_band, contract_last, preferred_element_type=F32) + bias_ref[variant, kvh]
            sink = sink_cols[kvh]
            m = jnp.maximum(jnp.max(s, axis=-1, keepdims=True), sink)
            p = jnp.exp(s - m).astype(BF16)
            o = jnp.dot(p, v_band, preferred_element_type=F32)
            esink = jnp.exp(sink - m)
            o_sw = pltpu.roll(o, A_HEAD_DIM, 1)
            even = o / (o_sw + esink)
            odd = o_sw / (o + esink)
            for g in range(0, A_GROUP, 2):
                pieces.append(jnp.where(low, even[g * BLOCK:(g + 1) * BLOCK], odd[(g + 1) * BLOCK:(g + 2) * BLOCK]))
        ao_ref[rows, :] = jnp.concatenate(pieces, axis=-1).astype(BF16)

        for hd in range(R_HEADS):
            q = qkr_ref[rows, hd * R_QK_DIM:(hd + 1) * R_QK_DIM]
            k = qkr_ref[rows, R_QK + hd * R_QK_DIM:R_QK + (hd + 1) * R_QK_DIM]
            vsl = slice(hd * R_V_DIM, (hd + 1) * R_V_DIM)
            v = vr_ref[rows, vsl]
            st = states[hd]
            qk = lax.dot_general(q, k, contract_last, preferred_element_type=F32) * dec_ref[hd]
            lhs = jnp.concatenate([qk.astype(BF16), (q.astype(F32) * xi_ref[hd]).astype(BF16)], axis=1)
            rhs = jnp.concatenate([v, st.astype(BF16)], axis=0)
            y = jnp.dot(lhs, rhs, preferred_element_type=F32)
            kz = (k.astype(F32) * zeta_ref[hd]).astype(BF16)
            kv = lax.dot_general(kz, v, contract_first, preferred_element_type=F32)
            states[hd] = gam_ref[hd] * st + kv
            mu = jnp.mean(y, axis=-1, keepdims=True)
            d = y - mu
            var = jnp.mean(d * d, axis=-1, keepdims=True)
            yn = d * lax.rsqrt(var + EPS) * rn_ref[:, vsl]
            gate = gr_ref[rows, vsl].astype(F32)
            ro_ref[rows, vsl] = (gate * jax.nn.sigmoid(gate) * yn).astype(BF16)

    for hd in range(R_HEADS):
        state_ref[hd] = states[hd]


def _mixer(qa, ka, vext, qkr, vr, gr, bias, sinks, ret_norm, tables, batch, seq):
    decay, zeta_b, xi_b, gamma_c = tables
    t = qa.shape[0]
    nch = min(MIX_CHUNKS, seq // CHUNK)
    tm = nch * CHUNK
    steps = seq // tm
    cur = lambda n: pl.BlockSpec((tm, n), lambda b, c: (b * steps + c, 0))
    prev = lambda n: pl.BlockSpec((CHUNK, n), lambda b, c: (b * steps * nch + jnp.maximum(c * nch - 1, 0), 0))
    smem = pl.BlockSpec(memory_space=pltpu.SMEM)
    return pl.pallas_call(
        functools.partial(_mixer_kernel, nch),
        grid=(batch, steps),
        in_specs=[smem, smem, cur(A_Q), cur(A_KV), prev(A_KV), cur(2 * LANES), prev(2 * LANES),
                  cur(2 * R_QK), cur(R_V), cur(R_V),
                  _resident((2, A_KV_HEADS, A_GROUP * BLOCK, 2 * BLOCK)),
                  _resident((R_HEADS, CHUNK, CHUNK)), _resident((R_HEADS, CHUNK, R_QK_DIM)),
                  _resident((R_HEADS, CHUNK, R_QK_DIM)), _resident((1, R_V))],
        out_specs=[cur(A_Q), cur(R_V)],
        scratch_shapes=[pltpu.VMEM((R_HEADS, R_QK_DIM, R_V_DIM), F32)],
        out_shape=[jax.ShapeDtypeStruct((t, A_Q), BF16), jax.ShapeDtypeStruct((t, R_V), BF16)],
        compiler_params=pltpu.CompilerParams(dimension_semantics=("parallel", "arbitrary"),
                                             vmem_limit_bytes=V7X_VMEM_LIMIT_BYTES),
        name="mixer",
    )(sinks.astype(F32), gamma_c, qa, ka, ka, vext, vext, qkr, vr, gr,
      bias.reshape(2, A_KV_HEADS, A_GROUP * BLOCK, 2 * BLOCK), decay, zeta_b, xi_b, ret_norm.reshape(1, R_V))


def _gelu_tanh(v):
    return 0.5 * v * (1.0 + jnp.tanh(math.sqrt(2.0 / math.pi) * (v + 0.044715 * (v * v * v))))


def _tail_kernel(tiles_per_seq, x_ref, ao_ref, ro_ref, gates_ref, woa_ref, wor_ref, wo_ref,
                 npm_ref, npf_ref, wup_ref, cw_ref, cb_ref, wdn_ref, nff_ref,
                 out_ref, carry_ref, hid_ref):
    tm = x_ref.shape[0]

    @pl.when(pl.program_id(0) % tiles_per_seq == 0)
    def _():
        carry_ref[...] = jnp.zeros_like(carry_ref)

    ya = jnp.dot(ao_ref[...], woa_ref[...], preferred_element_type=F32)
    yr = jnp.dot(ro_ref[...], wor_ref[...], preferred_element_type=F32)
    gate_a = jax.nn.sigmoid(gates_ref[:, :D_MODEL].astype(F32))
    gate_r = jax.nn.sigmoid(gates_ref[:, D_MODEL:].astype(F32))
    merged = (gate_a * ya + gate_r * yr).astype(BF16)
    x1 = x_ref[...] + _rms(jnp.dot(merged, wo_ref[...], preferred_element_type=F32), npm_ref[...])
    h = _rms(x1, npf_ref[...]).astype(BF16)

    row = lax.broadcasted_iota(jnp.int32, (tm, FF_CHUNK), 0)
    for c in range(D_FF // FF_CHUNK):
        cs = slice(c * FF_CHUNK, (c + 1) * FF_CHUNK)
        a = jnp.dot(h, wup_ref[:, cs], preferred_element_type=F32)
        val = jnp.dot(h, wup_ref[:, D_FF + c * FF_CHUNK:D_FF + (c + 1) * FF_CHUNK], preferred_element_type=F32)
        last = carry_ref[SUBLANES - 1:SUBLANES, cs]
        last2 = carry_ref[SUBLANES - 2:SUBLANES - 1, cs]
        a1 = jnp.where(row == 0, last, pltpu.roll(a, 1, 0))
        a2 = jnp.where(row == 0, last2, jnp.where(row == 1, last, pltpu.roll(a, 2, 0)))
        conv = cb_ref[:, cs] + cw_ref[0:1, cs] * a2 + cw_ref[1:2, cs] * a1 + cw_ref[2:3, cs] * a
        carry_ref[:, cs] = a[tm - SUBLANES:, :]
        hid_ref[:, cs] = (_gelu_tanh(conv) * val).astype(BF16)

    y = jnp.dot(hid_ref[...], wdn_ref[...], preferred_element_type=F32)
    out_ref[...] = x1 + _rms(y, nff_ref[...])


def _tail(xf, ao, ro, gates, woa, wor, wo, npm, npf, wup, cw, cb, wdn, nff, seq):
    t = xf.shape[0]
    tm = min(TM_TAIL, seq)
    row = lambda n: pl.BlockSpec((tm, n), lambda i: (i, 0))
    vec = lambda v: v.reshape(1, -1).astype(F32)
    return pl.pallas_call(
        functools.partial(_tail_kernel, seq // tm),
        grid=(t // tm,),
        in_specs=[row(D_MODEL), row(A_Q), row(R_V), row(2 * D_MODEL),
                  _resident((A_Q, D_MODEL)), _resident((R_V, D_MODEL)), _resident((D_MODEL, D_MODEL)),
                  _resident((1, D_MODEL)), _resident((1, D_MODEL)),
                  _resident((D_MODEL, 2 * D_FF)), _resident((CONV_WIDTH, D_FF)), _resident((1, D_FF)),
                  _resident((D_FF, D_MODEL)), _resident((1, D_MODEL))],
        out_specs=row(D_MODEL),
        out_shape=jax.ShapeDtypeStruct((t, D_MODEL), F32),
        scratch_shapes=[pltpu.VMEM((SUBLANES, D_FF), F32), pltpu.VMEM((tm, D_FF), BF16)],
        compiler_params=pltpu.CompilerParams(dimension_semantics=("arbitrary",),
                                             vmem_limit_bytes=V7X_VMEM_LIMIT_BYTES),
        name="tail",
    )(xf, ao, ro, gates, woa, wor, wo, vec(npm), vec(npf), wup, cw.astype(F32), vec(cb), wdn, vec(nff))


def kernel(x, norm_pre_mix, w_in, sinks, rel_bias, ret_norm, w_out_a, w_out_r, w_out, norm_post_mix,
           norm_pre_ffn, w_up, conv_w, conv_b, w_down, norm_post_ffn):
    batch, seq, _ = x.shape
    depth = w_in.shape[0]
    assert seq % (CHUNK * min(MIX_CHUNKS, seq // CHUNK)) == 0
    assert seq % min(TM_INPROJ, seq) == 0 and seq % min(TM_TAIL, seq) == 0
    xf = x.reshape(batch * seq, D_MODEL).astype(F32)
    cos, sin = _rotary_tables(seq)
    tables = _retention_tables()
    bias = _bias_table(rel_bias)
    w_in, w_out_a, w_out_r, w_out, w_up, w_down = (
        w.astype(BF16) for w in (w_in, w_out_a, w_out_r, w_out, w_up, w_down))
    for l in range(depth):
        qa, ka, vext, qkr, vr, gr, gates = _inproj(xf, norm_pre_mix[l].astype(F32), w_in[l], cos, sin, seq)
        ao, ro = _mixer(qa, ka, vext, qkr, vr, gr, bias, sinks[l], ret_norm[l].astype(F32), tables, batch, seq)
        xf = _tail(xf, ao, ro, gates, w_out_a[l], w_out_r[l], w_out[l], norm_post_mix[l], norm_pre_ffn[l],
                   w_up[l], conv_w[l], conv_b[l], w_down[l], norm_post_ffn[l], seq)
    return xf.reshape(batch, seq, D_MODEL).astype(x.dtype)
```

```python
import functools
import math

import numpy as np
import jax
import jax.numpy as jnp
from jax import lax
from jax.experimental import pallas as pl
from jax.experimental.pallas import tpu as pltpu

D_MODEL = 1024
A_HEADS = 8
A_KV_HEADS = 2
A_HEAD_DIM = 64
A_GROUP = A_HEADS // A_KV_HEADS
WINDOW = 128
BLOCK = 128
NUM_BUCKETS = 32
MAX_DISTANCE = 128
R_HEADS = 4
R_QK_DIM = 128
R_V_DIM = 256
ROPE_BASE = 10000.0
D_FF = 2816
CONV_WIDTH = 3
EPS = 1e-6
NEG_INF = -1e30
LOG2E = math.log2(math.e)

A_Q = A_HEADS * A_HEAD_DIM
A_KV = A_KV_HEADS * A_HEAD_DIM
R_QK = R_HEADS * R_QK_DIM
R_V = R_HEADS * R_V_DIM
OFF_AK = A_Q
OFF_AV = OFF_AK + A_KV
OFF_RQK = OFF_AV + A_KV
OFF_RV = OFF_RQK + 2 * R_QK
OFF_RG = OFF_RV + R_V
OFF_GATES = OFF_RG + R_V
D_IN = OFF_GATES + 2 * D_MODEL
NAT_AK = A_Q
NAT_RQK = NAT_AK + A_KV
NAT_GATES = NAT_RQK + 2 * R_QK
D_NAT = NAT_GATES + 2 * D_MODEL
TR_RV = A_KV
TR_RG = TR_RV + R_V
D_TR = TR_RG + R_V

V7X_VMEM_LIMIT_BYTES = 56 * 1024 * 1024
SUBLANES = 8
LANES = 128
FF_CHUNK = 256
TM_INPROJ = 512
INPROJ_SPLIT = 2
TM_TAIL = 512
TAIL_SPLIT = 2
MIX_TOKENS = 4 * BLOCK
RET_CHUNK = 2 * BLOCK

BF16 = jnp.bfloat16
F32 = jnp.float32


def _resident(shape):
    nd = len(shape)
    return pl.BlockSpec(shape, lambda *_: (0,) * nd, pipeline_mode=pl.Buffered(1))


def _rms(v, gain):
    return v * lax.rsqrt(jnp.mean(v * v, axis=-1, keepdims=True) + EPS) * gain


def _t5_band_buckets():
    i = np.arange(BLOCK)[:, None]
    j = np.arange(2 * BLOCK)[None, :]
    dist = BLOCK + i - j
    n = np.maximum(dist, 0)
    max_exact = NUM_BUCKETS // 2
    large = max_exact + (np.log(np.maximum(n, 1) / max_exact) / math.log(MAX_DISTANCE / max_exact)
                         * (NUM_BUCKETS - max_exact)).astype(np.int32)
    large = np.minimum(large, NUM_BUCKETS - 1)
    bucket = np.where(n < max_exact, n, large).astype(np.int32)
    valid = (dist >= 0) & (dist < WINDOW)
    return bucket, valid


def _rotary_tables(seq):
    half = R_QK_DIM // 2
    freqs = ROPE_BASE ** (-jnp.arange(half, dtype=F32) / half)
    ang = jnp.arange(seq).astype(F32)[:, None] * freqs[None, :]
    cos, sin = jnp.cos(ang), jnp.sin(ang)
    return jnp.concatenate([cos, cos], axis=-1), jnp.concatenate([-sin, sin], axis=-1)


def _retention_tables(chunk):
    log_gamma = jnp.log(1.0 - 2.0 ** (-5.0 - jnp.arange(R_HEADS, dtype=F32)))
    idx = jnp.arange(chunk, dtype=F32)
    rel = idx[None, :] - idx[:, None]
    decay_t = jnp.where(rel >= 0, jnp.exp(log_gamma[:, None, None] * jnp.maximum(rel, 0.0)), 0.0)
    zeta = jnp.exp(log_gamma[:, None] * (chunk - 1 - idx)[None, :])
    xi = jnp.exp(log_gamma[:, None] * (idx + 1.0)[None, :])
    gamma_c = jnp.exp(log_gamma * chunk)
    zeta_b = jnp.broadcast_to(zeta[:, :, None], (R_HEADS, chunk, R_QK_DIM))
    xi_b = jnp.broadcast_to(xi[:, :, None], (R_HEADS, chunk, R_QK_DIM))
    return decay_t, zeta_b, xi_b, gamma_c


def _bias_kernel(tab_ref, bucket_ref, valid_ref, out_ref):
    bucket = bucket_ref[...]
    valid = valid_ref[...] > 0
    in_cur = lax.broadcasted_iota(jnp.int32, (2 * BLOCK, BLOCK), 0) >= BLOCK
    for h in range(A_HEADS):
        kvh, g = divmod(h, A_GROUP)
        acc = jnp.zeros((2 * BLOCK, BLOCK), F32)
        for b in range(NUM_BUCKETS):
            acc = jnp.where(bucket == b, tab_ref[b, h] * LOG2E, acc)
        out_ref[1, kvh, :, g * BLOCK:(g + 1) * BLOCK] = jnp.where(valid, acc, NEG_INF)
        out_ref[0, kvh, :, g * BLOCK:(g + 1) * BLOCK] = jnp.where(valid & in_cur, acc, NEG_INF)


def _bias_table(rel_bias):
    bucket, valid = _t5_band_buckets()
    return pl.pallas_call(
        _bias_kernel,
        out_shape=jax.ShapeDtypeStruct((2, A_KV_HEADS, 2 * BLOCK, A_GROUP * BLOCK), F32),
        in_specs=[pl.BlockSpec(memory_space=pltpu.SMEM),
                  pl.BlockSpec(memory_space=pltpu.VMEM),
                  pl.BlockSpec(memory_space=pltpu.VMEM)],
        out_specs=pl.BlockSpec(memory_space=pltpu.VMEM),
        name="bias_table",
    )(rel_bias.astype(F32), jnp.asarray(bucket.T), jnp.asarray(valid.T.astype(np.int32)))


def _inproj_kernel(x_ref, g_ref, wn_ref, wt_ref, cos_ref, sin_ref,
                   qa_ref, ka_ref, vt_ref, qkr_ref, vrt_ref, gst_ref, gates_ref):
    tm = x_ref.shape[0]
    hm = tm // INPROJ_SPLIT
    groups = [slice(i * hm, (i + 1) * hm) for i in range(INPROJ_SPLIT)]
    hs = [_rms(x_ref[rs, :], g_ref[...]).astype(BF16) for rs in groups]
    ones = jnp.ones((A_HEAD_DIM, hm), BF16)

    for rs, h in zip(groups, hs):
        def proj(lo, hi):
            return jnp.dot(h, wn_ref[:, lo:hi], preferred_element_type=F32)

        def proj_t(lo, hi):
            return lax.dot_general(wt_ref[lo:hi, :], h, (((1,), (1,)), ((), ())), preferred_element_type=F32)

        qa_ref[rs, :] = (proj(0, NAT_AK) * (A_HEAD_DIM ** -0.5 * LOG2E)).astype(BF16)
        ka_ref[rs, :] = proj(NAT_AK, NAT_RQK).astype(BF16)
        vat = proj_t(0, TR_RV).astype(BF16)
        for kvh in range(A_KV_HEADS):
            vt_ref[kvh * LANES:kvh * LANES + A_HEAD_DIM, rs] = vat[kvh * A_HEAD_DIM:(kvh + 1) * A_HEAD_DIM, :]
            vt_ref[kvh * LANES + A_HEAD_DIM:(kvh + 1) * LANES, rs] = ones
        vrt_ref[:, rs] = proj_t(TR_RV, TR_RG).astype(BF16)
        g = proj_t(TR_RG, D_TR)
        gst_ref[:, rs] = (g * jax.nn.sigmoid(g)).astype(BF16)
        qk = proj(NAT_RQK, NAT_GATES)
        cos = cos_ref[rs, :]
        sin = sin_ref[rs, :]
        for j in range(2 * R_HEADS):
            xh = qk[:, j * R_QK_DIM:(j + 1) * R_QK_DIM]
            rot = xh * cos + pltpu.roll(xh, R_QK_DIM // 2, 1) * sin
            if j >= R_HEADS:
                rot = rot * (R_QK_DIM ** -0.5)
            qkr_ref[rs, j * R_QK_DIM:(j + 1) * R_QK_DIM] = rot.astype(BF16)
        gates_ref[rs, :] = proj(NAT_GATES, D_NAT).astype(BF16)


def _inproj(xf, gain, wn, wt, cos, sin, seq):
    t = xf.shape[0]
    tm = min(TM_INPROJ, seq)
    tiles_per_seq = seq // tm
    row = lambda n: pl.BlockSpec((tm, n), lambda i: (i, 0))
    col = lambda n: pl.BlockSpec((n, tm), lambda i: (0, i))
    pos = pl.BlockSpec((tm, R_QK_DIM), lambda i: (i % tiles_per_seq, 0))
    tok = lambda n: jax.ShapeDtypeStruct((t, n), BF16)
    feat = lambda n: jax.ShapeDtypeStruct((n, t), BF16)
    return pl.pallas_call(
        _inproj_kernel,
        grid=(t // tm,),
        in_specs=[row(D_MODEL), _resident((1, D_MODEL)), _resident((D_MODEL, D_NAT)), _resident((D_TR, D_MODEL)),
                  pos, pos],
        out_specs=[row(A_Q), row(A_KV), col(2 * LANES), row(2 * R_QK), col(R_V), col(R_V), row(2 * D_MODEL)],
        out_shape=[tok(A_Q), tok(A_KV), feat(2 * LANES), tok(2 * R_QK), feat(R_V), feat(R_V), tok(2 * D_MODEL)],
        compiler_params=pltpu.CompilerParams(dimension_semantics=("parallel",),
                                             vmem_limit_bytes=V7X_VMEM_LIMIT_BYTES),
        name="inproj",
    )(xf, gain.reshape(1, D_MODEL), wn, wt, cos, sin)


def _mixer_kernel(sink_ref, gam_ref, qa_ref, kc_ref, kp_ref, vt_ref, vtp_ref, qkr_ref, vrt_ref, gst_ref,
                  bias_ref, dec_ref, zeta_ref, xi_ref, rn_ref, aot_ref, rot_ref, state_ref):
    step = pl.program_id(1)

    @pl.when(step == 0)
    def _():
        state_ref[...] = jnp.zeros_like(state_ref)

    contract_last = (((1,), (1,)), ((), ()))

    lane_head = lax.broadcasted_iota(jnp.int32, (1, A_GROUP * BLOCK), 1) // BLOCK
    sink_rows = []
    for kvh in range(A_KV_HEADS):
        r = jnp.full((1, A_GROUP * BLOCK), sink_ref[kvh * A_GROUP] * LOG2E, F32)
        for g in range(1, A_GROUP):
            r = jnp.where(lane_head == g, sink_ref[kvh * A_GROUP + g] * LOG2E, r)
        sink_rows.append(r)

    def att_first(c, kvh):
        cur = slice(c * BLOCK, (c + 1) * BLOCK)
        prv = slice((c - 1) * BLOCK, c * BLOCK)
        ks = slice(kvh * A_HEAD_DIM, (kvh + 1) * A_HEAD_DIM)
        k_prev = kp_ref[:, ks] if c == 0 else kc_ref[prv, ks]
        k_band = jnp.concatenate([k_prev, kc_ref[cur, ks]], axis=0)
        q = jnp.concatenate([qa_ref[cur, (kvh * A_GROUP + g) * A_HEAD_DIM:(kvh * A_GROUP + g + 1) * A_HEAD_DIM]
                             for g in range(A_GROUP)], axis=0)
        return lax.dot_general(k_band, q, contract_last, preferred_element_type=F32)

    def att_second(c, kvh, s):
        cur = slice(c * BLOCK, (c + 1) * BLOCK)
        prv = slice((c - 1) * BLOCK, c * BLOCK)
        vs = slice(kvh * LANES, (kvh + 1) * LANES)
        variant = jnp.minimum(step, 1) if c == 0 else 1
        v_prev = vtp_ref[vs, :] if c == 0 else vt_ref[vs, prv]
        v_band = jnp.concatenate([v_prev, vt_ref[vs, cur]], axis=1)
        s = s + bias_ref[variant, kvh]
        sink = sink_rows[kvh]
        m = jnp.maximum(jnp.max(s, axis=0, keepdims=True), sink)
        p = jnp.exp2(s - m).astype(BF16)
        o = jnp.dot(v_band, p, preferred_element_type=F32)
        return o, jnp.exp2(sink - m)

    def att_third(c, kvh, second):
        o, esink = second
        cur = slice(c * BLOCK, (c + 1) * BLOCK)
        inv = 1.0 / (o[A_HEAD_DIM:A_HEAD_DIM + 1, :] + esink)
        res = (o[:A_HEAD_DIM, :] * inv).astype(BF16)
        for g in range(A_GROUP):
            hd = kvh * A_GROUP + g
            aot_ref[hd * A_HEAD_DIM:(hd + 1) * A_HEAD_DIM, cur] = res[:, g * BLOCK:(g + 1) * BLOCK]

    states = [state_ref[hd] for hd in range(R_HEADS)]

    def ret_first(c, hd):
        tok = slice(c * RET_CHUNK, (c + 1) * RET_CHUNK)
        q = qkr_ref[tok, hd * R_QK_DIM:(hd + 1) * R_QK_DIM]
        k = qkr_ref[tok, R_QK + hd * R_QK_DIM:R_QK + (hd + 1) * R_QK_DIM]
        return lax.dot_general(k, q, contract_last, preferred_element_type=F32)

    def ret_second(c, hd, a):
        tok = slice(c * RET_CHUNK, (c + 1) * RET_CHUNK)
        q = qkr_ref[tok, hd * R_QK_DIM:(hd + 1) * R_QK_DIM]
        k = qkr_ref[tok, R_QK + hd * R_QK_DIM:R_QK + (hd + 1) * R_QK_DIM]
        vt = vrt_ref[hd * R_V_DIM:(hd + 1) * R_V_DIM, tok]
        st = states[hd]
        a = (a * dec_ref[hd]).astype(BF16)
        qx = (q.astype(F32) * xi_ref[hd]).astype(BF16)
        y = (jnp.dot(vt, a, preferred_element_type=F32)
             + lax.dot_general(st.astype(BF16), qx, contract_last, preferred_element_type=F32))
        kz = (k.astype(F32) * zeta_ref[hd]).astype(BF16)
        states[hd] = gam_ref[hd] * st + jnp.dot(vt, kz, preferred_element_type=F32)
        return y

    def ret_third(c, hd, y):
        tok = slice(c * RET_CHUNK, (c + 1) * RET_CHUNK)
        vsl = slice(hd * R_V_DIM, (hd + 1) * R_V_DIM)
        mu = jnp.mean(y, axis=0, keepdims=True)
        d = y - mu
        var = jnp.mean(d * d, axis=0, keepdims=True)
        yn = d * lax.rsqrt(var + EPS) * rn_ref[vsl, :]
        rot_ref[vsl, tok] = (gst_ref[vsl, tok].astype(F32) * yn).astype(BF16)

    att = [(att_first, att_second, att_third, c, kvh)
           for c in range(MIX_TOKENS // BLOCK) for kvh in range(A_KV_HEADS)]
    ret = [(ret_first, ret_second, ret_third, c, hd) for c in range(MIX_TOKENS // RET_CHUNK) for hd in range(R_HEADS)]
    items = [it for pair in zip(att, ret) for it in pair]
    firsts, seconds = {}, {}
    for i in range(len(items) + 2):
        if i < len(items):
            f1, _, _, c, j = items[i]
            firsts[i] = f1(c, j)
        if 0 <= i - 1 < len(items):
            _, f2, _, c, j = items[i - 1]
            seconds[i - 1] = f2(c, j, firsts.pop(i - 1))
        if 0 <= i - 2 < len(items):
            _, _, f3, c, j = items[i - 2]
            f3(c, j, seconds.pop(i - 2))
    for hd in range(R_HEADS):
        state_ref[hd] = states[hd]


def _mixer(qa, ka, vt, qkr, vrt, gst, bias, sinks, ret_norm, tables, batch, seq):
    decay_t, zeta_b, xi_b, gamma_c = tables
    t = qa.shape[0]
    tm = MIX_TOKENS
    steps = seq // tm
    nb = tm // BLOCK
    row = lambda n: pl.BlockSpec((tm, n), lambda b, c: (b * steps + c, 0))
    col = lambda n: pl.BlockSpec((n, tm), lambda b, c: (0, b * steps + c))
    prev_block = lambda b, c: b * steps * nb + jnp.maximum(c * nb - 1, 0)
    prow = pl.BlockSpec((BLOCK, A_KV), lambda b, c: (prev_block(b, c), 0))
    pcol = pl.BlockSpec((2 * LANES, BLOCK), lambda b, c: (0, prev_block(b, c)))
    smem = pl.BlockSpec(memory_space=pltpu.SMEM)
    gain_b = jnp.broadcast_to(ret_norm.reshape(R_V, 1), (R_V, RET_CHUNK))
    return pl.pallas_call(
        _mixer_kernel,
        grid=(batch, steps),
        in_specs=[smem, smem, row(A_Q), row(A_KV), prow, col(2 * LANES), pcol, row(2 * R_QK), col(R_V), col(R_V),
                  _resident((2, A_KV_HEADS, 2 * BLOCK, A_GROUP * BLOCK)),
                  _resident((R_HEADS, RET_CHUNK, RET_CHUNK)), _resident((R_HEADS, RET_CHUNK, R_QK_DIM)),
                  _resident((R_HEADS, RET_CHUNK, R_QK_DIM)), _resident((R_V, RET_CHUNK))],
        out_specs=[col(A_Q), col(R_V)],
        scratch_shapes=[pltpu.VMEM((R_HEADS, R_V_DIM, R_QK_DIM), F32)],
        out_shape=[jax.ShapeDtypeStruct((A_Q, t), BF16), jax.ShapeDtypeStruct((R_V, t), BF16)],
        compiler_params=pltpu.CompilerParams(dimension_semantics=("parallel", "arbitrary"),
                                             vmem_limit_bytes=V7X_VMEM_LIMIT_BYTES),
        name="mixer",
    )(sinks.astype(F32), gamma_c, qa, ka, ka, vt, vt, qkr, vrt, gst, bias, decay_t, zeta_b, xi_b, gain_b)


def _gelu_tanh(v):
    return 0.5 * v * (1.0 + jnp.tanh(math.sqrt(2.0 / math.pi) * (v + 0.044715 * (v * v * v))))


def _tail_kernel(tiles_per_seq, x_ref, aot_ref, rot_ref, gates_ref, woa_ref, wor_ref, wo_ref,
                 npm_ref, npf_ref, wup_ref, cw_ref, cb_ref, wdn_ref, nff_ref,
                 out_ref, carry_ref, hid_ref):
    tm = x_ref.shape[0]
    hm = tm // TAIL_SPLIT
    contract_first = (((0,), (0,)), ((), ()))

    @pl.when(pl.program_id(0) % tiles_per_seq == 0)
    def _():
        carry_ref[...] = jnp.zeros_like(carry_ref)

    def mix_out(rs):
        ya = lax.dot_general(aot_ref[:, rs], woa_ref[...], contract_first, preferred_element_type=F32)
        yr = lax.dot_general(rot_ref[:, rs], wor_ref[...], contract_first, preferred_element_type=F32)
        gate_a = jax.nn.sigmoid(gates_ref[rs, :D_MODEL].astype(F32))
        gate_r = jax.nn.sigmoid(gates_ref[rs, D_MODEL:].astype(F32))
        merged = (gate_a * ya + gate_r * yr).astype(BF16)
        x1 = x_ref[rs, :] + _rms(jnp.dot(merged, wo_ref[...], preferred_element_type=F32), npm_ref[...])
        return x1, _rms(x1, npf_ref[...]).astype(BF16)

    def ffn(rs, x1, h):
        row = lax.broadcasted_iota(jnp.int32, (hm, FF_CHUNK), 0)
        for c in range(D_FF // FF_CHUNK):
            cs = slice(c * FF_CHUNK, (c + 1) * FF_CHUNK)
            a = jnp.dot(h, wup_ref[:, cs], preferred_element_type=F32)
            val = jnp.dot(h, wup_ref[:, D_FF + c * FF_CHUNK:D_FF + (c + 1) * FF_CHUNK], preferred_element_type=F32)
            last = carry_ref[SUBLANES - 1:SUBLANES, cs]
            last2 = carry_ref[SUBLANES - 2:SUBLANES - 1, cs]
            a1 = jnp.where(row == 0, last, pltpu.roll(a, 1, 0))
            a2 = jnp.where(row == 0, last2, jnp.where(row == 1, last, pltpu.roll(a, 2, 0)))
            conv = cb_ref[:, cs] + cw_ref[0:1, cs] * a2 + cw_ref[1:2, cs] * a1 + cw_ref[2:3, cs] * a
            carry_ref[:, cs] = a[hm - SUBLANES:, :]
            hid_ref[rs, cs] = (_gelu_tanh(conv) * val).astype(BF16)
        y = jnp.dot(hid_ref[rs, :], wdn_ref[...], preferred_element_type=F32)
        out_ref[rs, :] = x1 + _rms(y, nff_ref[...])

    groups = [slice(i * hm, (i + 1) * hm) for i in range(TAIL_SPLIT)]
    mixed = [mix_out(rs) for rs in groups]
    for rs, (x1, h) in zip(groups, mixed):
        ffn(rs, x1, h)


def _tail(xf, aot, rot, gates, woa, wor, wo, npm, npf, wup, cw, cb, wdn, nff, seq):
    t = xf.shape[0]
    tm = min(TM_TAIL, seq)
    row = lambda n: pl.BlockSpec((tm, n), lambda i: (i, 0))
    col = lambda n: pl.BlockSpec((n, tm), lambda i: (0, i))
    vec = lambda v: v.reshape(1, -1).astype(F32)
    return pl.pallas_call(
        functools.partial(_tail_kernel, seq // tm),
        grid=(t // tm,),
        in_specs=[row(D_MODEL), col(A_Q), col(R_V), row(2 * D_MODEL),
                  _resident((A_Q, D_MODEL)), _resident((R_V, D_MODEL)), _resident((D_MODEL, D_MODEL)),
                  _resident((1, D_MODEL)), _resident((1, D_MODEL)),
                  _resident((D_MODEL, 2 * D_FF)), _resident((CONV_WIDTH, D_FF)), _resident((1, D_FF)),
                  _resident((D_FF, D_MODEL)), _resident((1, D_MODEL))],
        out_specs=row(D_MODEL),
        out_shape=jax.ShapeDtypeStruct((t, D_MODEL), F32),
        scratch_shapes=[pltpu.VMEM((SUBLANES, D_FF), F32), pltpu.VMEM((tm, D_FF), BF16)],
        compiler_params=pltpu.CompilerParams(dimension_semantics=("arbitrary",),
                                             vmem_limit_bytes=V7X_VMEM_LIMIT_BYTES),
        name="tail",
    )(xf, aot, rot, gates, woa, wor, wo, vec(npm), vec(npf), wup, cw.astype(F32), vec(cb), wdn, vec(nff))


def kernel(x, norm_pre_mix, w_in, sinks, rel_bias, ret_norm, w_out_a, w_out_r, w_out, norm_post_mix,
           norm_pre_ffn, w_up, conv_w, conv_b, w_down, norm_post_ffn):
    batch, seq, _ = x.shape
    depth = w_in.shape[0]
    assert seq % MIX_TOKENS == 0 and seq % min(TM_INPROJ, seq) == 0 and seq % min(TM_TAIL, seq) == 0
    xf = x.reshape(batch * seq, D_MODEL).astype(F32)
    cos, sin = _rotary_tables(seq)
    tables = _retention_tables(RET_CHUNK)
    bias = _bias_table(rel_bias)
    w_nat = jnp.concatenate([w_in[:, :, :OFF_AV], w_in[:, :, OFF_RQK:OFF_RV], w_in[:, :, OFF_GATES:]],
                            axis=2).astype(BF16)
    w_tr = jnp.swapaxes(jnp.concatenate([w_in[:, :, OFF_AV:OFF_RQK], w_in[:, :, OFF_RV:OFF_GATES]], axis=2),
                        1, 2).astype(BF16)
    w_out_a, w_out_r, w_out, w_up, w_down = (w.astype(BF16) for w in (w_out_a, w_out_r, w_out, w_up, w_down))
    for l in range(depth):
        qa, ka, vt, qkr, vrt, gst, gates = _inproj(xf, norm_pre_mix[l].astype(F32), w_nat[l], w_tr[l], cos, sin, seq)
        aot, rot = _mixer(qa, ka, vt, qkr, vrt, gst, bias, sinks[l], ret_norm[l].astype(F32), tables, batch, seq)
        xf = _tail(xf, aot, rot, gates, w_out_a[l], w_out_r[l], w_out[l], norm_post_mix[l], norm_pre_ffn[l],
                   w_up[l], conv_w[l], conv_b[l], w_down[l], norm_post_ffn[l], seq)
    return xf.reshape(batch, seq, D_MODEL).astype(x.dtype)
```

```python
import functools
import math

import numpy as np
import jax
import jax.numpy as jnp
from jax import lax
from jax.experimental import pallas as pl
from jax.experimental.pallas import tpu as pltpu

D_MODEL = 1024
A_HEADS = 8
A_KV_HEADS = 2
A_HEAD_DIM = 64
A_GROUP = A_HEADS // A_KV_HEADS
WINDOW = 128
BLOCK = 128
NUM_BUCKETS = 32
MAX_DISTANCE = 128
R_HEADS = 4
R_QK_DIM = 128
R_V_DIM = 256
ROPE_BASE = 10000.0
D_FF = 2816
CONV_WIDTH = 3
EPS = 1e-6
NEG_INF = -1e30
LOG2E = math.log2(math.e)

A_Q = A_HEADS * A_HEAD_DIM
A_KV = A_KV_HEADS * A_HEAD_DIM
R_QK = R_HEADS * R_QK_DIM
R_V = R_HEADS * R_V_DIM
OFF_AK = A_Q
OFF_AV = OFF_AK + A_KV
OFF_RQK = OFF_AV + A_KV
OFF_RV = OFF_RQK + 2 * R_QK
OFF_RG = OFF_RV + R_V
OFF_GATES = OFF_RG + R_V
D_IN = OFF_GATES + 2 * D_MODEL
V7X_VMEM_LIMIT_BYTES = 56 * 1024 * 1024
SUBLANES = 8
LANES = 128
FF_CHUNK = 256
TM_INPROJ = 512
INPROJ_SPLIT = 2
TM_TAIL = 512
TAIL_SPLIT = 2
MIX_TOKENS = 4 * BLOCK
RET_CHUNK = 2 * BLOCK

BF16 = jnp.bfloat16
F32 = jnp.float32


def _resident(shape):
    nd = len(shape)
    return pl.BlockSpec(shape, lambda *_: (0,) * nd, pipeline_mode=pl.Buffered(1))


def _layer_resident(shape, layer):
    nd = len(shape)
    return pl.BlockSpec((None,) + tuple(shape), lambda *_: (layer,) + (0,) * nd, pipeline_mode=pl.Buffered(1))


def _rms(v, gain):
    return v * lax.rsqrt(jnp.mean(v * v, axis=-1, keepdims=True) + EPS) * gain


def _t5_band_buckets():
    i = np.arange(BLOCK)[:, None]
    j = np.arange(2 * BLOCK)[None, :]
    dist = BLOCK + i - j
    n = np.maximum(dist, 0)
    max_exact = NUM_BUCKETS // 2
    large = max_exact + (np.log(np.maximum(n, 1) / max_exact) / math.log(MAX_DISTANCE / max_exact)
                         * (NUM_BUCKETS - max_exact)).astype(np.int32)
    large = np.minimum(large, NUM_BUCKETS - 1)
    bucket = np.where(n < max_exact, n, large).astype(np.int32)
    valid = (dist >= 0) & (dist < WINDOW)
    return bucket, valid


def _rotary_tables(seq):
    half = R_QK_DIM // 2
    freqs = ROPE_BASE ** (-jnp.arange(half, dtype=F32) / half)
    ang = jnp.arange(seq).astype(F32)[:, None] * freqs[None, :]
    cos, sin = jnp.cos(ang), jnp.sin(ang)
    return jnp.concatenate([cos, cos], axis=-1), jnp.concatenate([-sin, sin], axis=-1)


def _retention_tables(chunk):
    log_gamma = jnp.log(1.0 - 2.0 ** (-5.0 - jnp.arange(R_HEADS, dtype=F32)))
    idx = jnp.arange(chunk, dtype=F32)
    rel = idx[None, :] - idx[:, None]
    decay_t = jnp.where(rel >= 0, jnp.exp(log_gamma[:, None, None] * jnp.maximum(rel, 0.0)), 0.0)
    zeta = jnp.exp(log_gamma[:, None] * (chunk - 1 - idx)[None, :])
    xi = jnp.exp(log_gamma[:, None] * (idx + 1.0)[None, :])
    gamma_c = jnp.exp(log_gamma * chunk)
    zeta_b = jnp.broadcast_to(zeta[:, :, None], (R_HEADS, chunk, R_QK_DIM))
    xi_b = jnp.broadcast_to(xi[:, :, None], (R_HEADS, chunk, R_QK_DIM))
    return decay_t, zeta_b, xi_b, gamma_c


def _bias_kernel(tab_ref, bucket_ref, valid_ref, out_ref):
    bucket = bucket_ref[...]
    valid = valid_ref[...] > 0
    in_cur = lax.broadcasted_iota(jnp.int32, (2 * BLOCK, BLOCK), 0) >= BLOCK
    for h in range(A_HEADS):
        kvh, g = divmod(h, A_GROUP)
        acc = jnp.zeros((2 * BLOCK, BLOCK), F32)
        for b in range(NUM_BUCKETS):
            acc = jnp.where(bucket == b, tab_ref[b, h] * LOG2E, acc)
        out_ref[1, kvh, :, g * BLOCK:(g + 1) * BLOCK] = jnp.where(valid, acc, NEG_INF)
        out_ref[0, kvh, :, g * BLOCK:(g + 1) * BLOCK] = jnp.where(valid & in_cur, acc, NEG_INF)


def _bias_table(rel_bias):
    bucket, valid = _t5_band_buckets()
    return pl.pallas_call(
        _bias_kernel,
        out_shape=jax.ShapeDtypeStruct((2, A_KV_HEADS, 2 * BLOCK, A_GROUP * BLOCK), F32),
        in_specs=[pl.BlockSpec(memory_space=pltpu.SMEM),
                  pl.BlockSpec(memory_space=pltpu.VMEM),
                  pl.BlockSpec(memory_space=pltpu.VMEM)],
        out_specs=pl.BlockSpec(memory_space=pltpu.VMEM),
        name="bias_table",
    )(rel_bias.astype(F32), jnp.asarray(bucket.T), jnp.asarray(valid.T.astype(np.int32)))


def _inproj_kernel(x_ref, g_ref, w_ref, rn_ref, cos_ref, sin_ref,
                   qa_ref, ka_ref, vt_ref, qkr_ref, vrt_ref, gst_ref, gates_ref):
    tm = x_ref.shape[0]
    hm = tm // INPROJ_SPLIT
    groups = [slice(i * hm, (i + 1) * hm) for i in range(INPROJ_SPLIT)]
    hs = [_rms(x_ref[rs, :], g_ref[...]).astype(BF16) for rs in groups]
    ones = jnp.ones((A_HEAD_DIM, hm), BF16)

    for rs, h in zip(groups, hs):
        def proj(lo, hi):
            return jnp.dot(h, w_ref[:, lo:hi], preferred_element_type=F32)

        qa_ref[rs, :] = (proj(0, OFF_AK) * (A_HEAD_DIM ** -0.5 * LOG2E)).astype(BF16)
        ka_ref[rs, :] = proj(OFF_AK, OFF_AV).astype(BF16)
        vat = proj(OFF_AV, OFF_RQK).astype(BF16).T
        for kvh in range(A_KV_HEADS):
            vt_ref[kvh * LANES:kvh * LANES + A_HEAD_DIM, rs] = vat[kvh * A_HEAD_DIM:(kvh + 1) * A_HEAD_DIM, :]
            vt_ref[kvh * LANES + A_HEAD_DIM:(kvh + 1) * LANES, rs] = ones
        vrt_ref[:, rs] = proj(OFF_RV, OFF_RG).astype(BF16).T
        g = proj(OFF_RG, OFF_GATES)
        gst_ref[:, rs] = (g * jax.nn.sigmoid(g) * rn_ref[...]).astype(BF16).T
        qk = proj(OFF_RQK, OFF_RV)
        cos = cos_ref[rs, :]
        sin = sin_ref[rs, :]
        for j in range(2 * R_HEADS):
            xh = qk[:, j * R_QK_DIM:(j + 1) * R_QK_DIM]
            rot = xh * cos + pltpu.roll(xh, R_QK_DIM // 2, 1) * sin
            if j >= R_HEADS:
                rot = rot * (R_QK_DIM ** -0.5)
            qkr_ref[rs, j * R_QK_DIM:(j + 1) * R_QK_DIM] = rot.astype(BF16)
        gates_ref[rs, :] = proj(OFF_GATES, D_IN).astype(BF16)


def _inproj(xf, gain, w, layer, ret_norm, cos, sin, seq):
    t = xf.shape[0]
    tm = min(TM_INPROJ, seq)
    tiles_per_seq = seq // tm
    row = lambda n: pl.BlockSpec((tm, n), lambda i: (i, 0))
    col = lambda n: pl.BlockSpec((n, tm), lambda i: (0, i))
    pos = pl.BlockSpec((tm, R_QK_DIM), lambda i: (i % tiles_per_seq, 0))
    tok = lambda n: jax.ShapeDtypeStruct((t, n), BF16)
    feat = lambda n: jax.ShapeDtypeStruct((n, t), BF16)
    return pl.pallas_call(
        _inproj_kernel,
        grid=(t // tm,),
        in_specs=[row(D_MODEL), _resident((1, D_MODEL)), _layer_resident((D_MODEL, D_IN), layer),
                  _resident((1, R_V)), pos, pos],
        out_specs=[row(A_Q), row(A_KV), col(2 * LANES), row(2 * R_QK), col(R_V), col(R_V), row(2 * D_MODEL)],
        out_shape=[tok(A_Q), tok(A_KV), feat(2 * LANES), tok(2 * R_QK), feat(R_V), feat(R_V), tok(2 * D_MODEL)],
        compiler_params=pltpu.CompilerParams(dimension_semantics=("parallel",),
                                             vmem_limit_bytes=V7X_VMEM_LIMIT_BYTES),
        name="inproj",
    )(xf, gain.reshape(1, D_MODEL), w, ret_norm.reshape(1, R_V), cos, sin)


def _mixer_kernel(sink_ref, gam_ref, qa_ref, kc_ref, kp_ref, vt_ref, vtp_ref, qkr_ref, vrt_ref, gst_ref,
                  bias_ref, dec_ref, zeta_ref, xi_ref, aot_ref, rot_ref, state_ref):
    step = pl.program_id(1)

    @pl.when(step == 0)
    def _():
        state_ref[...] = jnp.zeros_like(state_ref)

    contract_last = (((1,), (1,)), ((), ()))

    lane_head = lax.broadcasted_iota(jnp.int32, (1, A_GROUP * BLOCK), 1) // BLOCK
    sink_rows = []
    for kvh in range(A_KV_HEADS):
        r = jnp.full((1, A_GROUP * BLOCK), sink_ref[kvh * A_GROUP] * LOG2E, F32)
        for g in range(1, A_GROUP):
            r = jnp.where(lane_head == g, sink_ref[kvh * A_GROUP + g] * LOG2E, r)
        sink_rows.append(r)

    def att_first(c, kvh):
        cur = slice(c * BLOCK, (c + 1) * BLOCK)
        prv = slice((c - 1) * BLOCK, c * BLOCK)
        ks = slice(kvh * A_HEAD_DIM, (kvh + 1) * A_HEAD_DIM)
        k_prev = kp_ref[:, ks] if c == 0 else kc_ref[prv, ks]
        k_band = jnp.concatenate([k_prev, kc_ref[cur, ks]], axis=0)
        q = jnp.concatenate([qa_ref[cur, (kvh * A_GROUP + g) * A_HEAD_DIM:(kvh * A_GROUP + g + 1) * A_HEAD_DIM]
                             for g in range(A_GROUP)], axis=0)
        return lax.dot_general(k_band, q, contract_last, preferred_element_type=F32)

    def att_second(c, kvh, s):
        cur = slice(c * BLOCK, (c + 1) * BLOCK)
        prv = slice((c - 1) * BLOCK, c * BLOCK)
        vs = slice(kvh * LANES, (kvh + 1) * LANES)
        variant = jnp.minimum(step, 1) if c == 0 else 1
        v_prev = vtp_ref[vs, :] if c == 0 else vt_ref[vs, prv]
        v_band = jnp.concatenate([v_prev, vt_ref[vs, cur]], axis=1)
        s = s + bias_ref[variant, kvh]
        sink = sink_rows[kvh]
        m = jnp.maximum(jnp.max(s, axis=0, keepdims=True), sink)
        p = jnp.exp2(s - m).astype(BF16)
        o = jnp.dot(v_band, p, preferred_element_type=F32)
        return o, jnp.exp2(sink - m)

    def att_third(c, kvh, second):
        o, esink = second
        cur = slice(c * BLOCK, (c + 1) * BLOCK)
        inv = 1.0 / (o[A_HEAD_DIM:A_HEAD_DIM + 1, :] + esink)
        res = (o[:A_HEAD_DIM, :] * inv).astype(BF16)
        for g in range(A_GROUP):
            hd = kvh * A_GROUP + g
            aot_ref[hd * A_HEAD_DIM:(hd + 1) * A_HEAD_DIM, cur] = res[:, g * BLOCK:(g + 1) * BLOCK]

    states = [state_ref[hd] for hd in range(R_HEADS)]

    def ret_first(c, hd):
        tok = slice(c * RET_CHUNK, (c + 1) * RET_CHUNK)
        q = qkr_ref[tok, hd * R_QK_DIM:(hd + 1) * R_QK_DIM]
        k = qkr_ref[tok, R_QK + hd * R_QK_DIM:R_QK + (hd + 1) * R_QK_DIM]
        return lax.dot_general(k, q, contract_last, preferred_element_type=F32)

    def ret_second(c, hd, a):
        tok = slice(c * RET_CHUNK, (c + 1) * RET_CHUNK)
        q = qkr_ref[tok, hd * R_QK_DIM:(hd + 1) * R_QK_DIM]
        k = qkr_ref[tok, R_QK + hd * R_QK_DIM:R_QK + (hd + 1) * R_QK_DIM]
        vt = vrt_ref[hd * R_V_DIM:(hd + 1) * R_V_DIM, tok]
        st = states[hd]
        a = (a * dec_ref[hd]).astype(BF16)
        qx = (q.astype(F32) * xi_ref[hd]).astype(BF16)
        y = (jnp.dot(vt, a, preferred_element_type=F32)
             + lax.dot_general(st.astype(BF16), qx, contract_last, preferred_element_type=F32))
        kz = (k.astype(F32) * zeta_ref[hd]).astype(BF16)
        states[hd] = gam_ref[hd] * st + jnp.dot(vt, kz, preferred_element_type=F32)
        return y

    def ret_third(c, hd, y):
        tok = slice(c * RET_CHUNK, (c + 1) * RET_CHUNK)
        vsl = slice(hd * R_V_DIM, (hd + 1) * R_V_DIM)
        mu = jnp.mean(y, axis=0, keepdims=True)
        d = y - mu
        var = jnp.mean(d * d, axis=0, keepdims=True)
        yn = d * lax.rsqrt(var + EPS)
        rot_ref[vsl, tok] = (gst_ref[vsl, tok].astype(F32) * yn).astype(BF16)

    att = [(att_first, att_second, att_third, c, kvh)
           for c in range(MIX_TOKENS // BLOCK) for kvh in range(A_KV_HEADS)]
    ret = [(ret_first, ret_second, ret_third, c, hd) for c in range(MIX_TOKENS // RET_CHUNK) for hd in range(R_HEADS)]
    items = [it for pair in zip(att, ret) for it in pair]
    firsts, seconds = {}, {}
    for i in range(len(items) + 2):
        if i < len(items):
            f1, _, _, c, j = items[i]
            firsts[i] = f1(c, j)
        if 0 <= i - 1 < len(items):
            _, f2, _, c, j = items[i - 1]
            seconds[i - 1] = f2(c, j, firsts.pop(i - 1))
        if 0 <= i - 2 < len(items):
            _, _, f3, c, j = items[i - 2]
            f3(c, j, seconds.pop(i - 2))
    for hd in range(R_HEADS):
        state_ref[hd] = states[hd]


def _mixer(qa, ka, vt, qkr, vrt, gst, bias, sinks, tables, batch, seq):
    decay_t, zeta_b, xi_b, gamma_c = tables
    t = qa.shape[0]
    tm = MIX_TOKENS
    steps = seq // tm
    nb = tm // BLOCK
    row = lambda n: pl.BlockSpec((tm, n), lambda b, c: (b * steps + c, 0))
    col = lambda n: pl.BlockSpec((n, tm), lambda b, c: (0, b * steps + c))
    prev_block = lambda b, c: b * steps * nb + jnp.maximum(c * nb - 1, 0)
    prow = pl.BlockSpec((BLOCK, A_KV), lambda b, c: (prev_block(b, c), 0))
    pcol = pl.BlockSpec((2 * LANES, BLOCK), lambda b, c: (0, prev_block(b, c)))
    smem = pl.BlockSpec(memory_space=pltpu.SMEM)
    return pl.pallas_call(
        _mixer_kernel,
        grid=(batch, steps),
        in_specs=[smem, smem, row(A_Q), row(A_KV), prow, col(2 * LANES), pcol, row(2 * R_QK), col(R_V), col(R_V),
                  _resident((2, A_KV_HEADS, 2 * BLOCK, A_GROUP * BLOCK)),
                  _resident((R_HEADS, RET_CHUNK, RET_CHUNK)), _resident((R_HEADS, RET_CHUNK, R_QK_DIM)),
                  _resident((R_HEADS, RET_CHUNK, R_QK_DIM))],
        out_specs=[col(A_Q), col(R_V)],
        scratch_shapes=[pltpu.VMEM((R_HEADS, R_V_DIM, R_QK_DIM), F32)],
        out_shape=[jax.ShapeDtypeStruct((A_Q, t), BF16), jax.ShapeDtypeStruct((R_V, t), BF16)],
        compiler_params=pltpu.CompilerParams(dimension_semantics=("parallel", "arbitrary"),
                                             vmem_limit_bytes=V7X_VMEM_LIMIT_BYTES),
        name="mixer",
    )(sinks.astype(F32), gamma_c, qa, ka, ka, vt, vt, qkr, vrt, gst, bias, decay_t, zeta_b, xi_b)


def _gelu_tanh(v):
    return 0.5 * v * (1.0 + jnp.tanh(math.sqrt(2.0 / math.pi) * (v + 0.044715 * (v * v * v))))


def _tail_kernel(tiles_per_seq, x_ref, aot_ref, rot_ref, gates_ref, woa_ref, wor_ref, wo_ref,
                 npm_ref, npf_ref, wup_ref, cw_ref, cb_ref, wdn_ref, nff_ref,
                 out_ref, carry_ref, hid_ref):
    tm = x_ref.shape[0]
    hm = tm // TAIL_SPLIT
    contract_first = (((0,), (0,)), ((), ()))

    @pl.when(pl.program_id(0) % tiles_per_seq == 0)
    def _():
        carry_ref[...] = jnp.zeros_like(carry_ref)

    def mix_out(rs):
        ya = lax.dot_general(aot_ref[:, rs], woa_ref[...], contract_first, preferred_element_type=F32)
        yr = lax.dot_general(rot_ref[:, rs], wor_ref[...], contract_first, preferred_element_type=F32)
        gate_a = jax.nn.sigmoid(gates_ref[rs, :D_MODEL].astype(F32))
        gate_r = jax.nn.sigmoid(gates_ref[rs, D_MODEL:].astype(F32))
        merged = (gate_a * ya + gate_r * yr).astype(BF16)
        x1 = x_ref[rs, :] + _rms(jnp.dot(merged, wo_ref[...], preferred_element_type=F32), npm_ref[...])
        return x1, _rms(x1, npf_ref[...]).astype(BF16)

    def ffn(rs, x1, h):
        row = lax.broadcasted_iota(jnp.int32, (hm, FF_CHUNK), 0)
        for c in range(D_FF // FF_CHUNK):
            cs = slice(c * FF_CHUNK, (c + 1) * FF_CHUNK)
            a = jnp.dot(h, wup_ref[:, cs], preferred_element_type=F32)
            val = jnp.dot(h, wup_ref[:, D_FF + c * FF_CHUNK:D_FF + (c + 1) * FF_CHUNK], preferred_element_type=F32)
            last = carry_ref[SUBLANES - 1:SUBLANES, cs]
            last2 = carry_ref[SUBLANES - 2:SUBLANES - 1, cs]
            a1 = jnp.where(row == 0, last, pltpu.roll(a, 1, 0))
            a2 = jnp.where(row == 0, last2, jnp.where(row == 1, last, pltpu.roll(a, 2, 0)))
            conv = cb_ref[:, cs] + cw_ref[0:1, cs] * a2 + cw_ref[1:2, cs] * a1 + cw_ref[2:3, cs] * a
            carry_ref[:, cs] = a[hm - SUBLANES:, :]
            hid_ref[rs, cs] = (_gelu_tanh(conv) * val).astype(BF16)
        y = jnp.dot(hid_ref[rs, :], wdn_ref[...], preferred_element_type=F32)
        out_ref[rs, :] = x1 + _rms(y, nff_ref[...])

    groups = [slice(i * hm, (i + 1) * hm) for i in range(TAIL_SPLIT)]
    mixed = [mix_out(rs) for rs in groups]
    for rs, (x1, h) in zip(groups, mixed):
        ffn(rs, x1, h)


def _tail(xf, aot, rot, gates, woa, wor, wo, npm, npf, wup, cw, cb, wdn, nff, layer, seq):
    t = xf.shape[0]
    tm = min(TM_TAIL, seq)
    row = lambda n: pl.BlockSpec((tm, n), lambda i: (i, 0))
    col = lambda n: pl.BlockSpec((n, tm), lambda i: (0, i))
    vec = lambda v: v.reshape(1, -1).astype(F32)
    return pl.pallas_call(
        functools.partial(_tail_kernel, seq // tm),
        grid=(t // tm,),
        in_specs=[row(D_MODEL), col(A_Q), col(R_V), row(2 * D_MODEL),
                  _layer_resident((A_Q, D_MODEL), layer), _layer_resident((R_V, D_MODEL), layer),
                  _layer_resident((D_MODEL, D_MODEL), layer),
                  _resident((1, D_MODEL)), _resident((1, D_MODEL)),
                  _layer_resident((D_MODEL, 2 * D_FF), layer), _resident((CONV_WIDTH, D_FF)), _resident((1, D_FF)),
                  _layer_resident((D_FF, D_MODEL), layer), _resident((1, D_MODEL))],
        out_specs=row(D_MODEL),
        out_shape=jax.ShapeDtypeStruct((t, D_MODEL), F32),
        scratch_shapes=[pltpu.VMEM((SUBLANES, D_FF), F32), pltpu.VMEM((tm, D_FF), BF16)],
        compiler_params=pltpu.CompilerParams(dimension_semantics=("arbitrary",),
                                             vmem_limit_bytes=V7X_VMEM_LIMIT_BYTES),
        name="tail",
    )(xf, aot, rot, gates, woa, wor, wo, vec(npm), vec(npf), wup, cw.astype(F32), vec(cb), wdn, vec(nff))


def kernel(x, norm_pre_mix, w_in, sinks, rel_bias, ret_norm, w_out_a, w_out_r, w_out, norm_post_mix,
           norm_pre_ffn, w_up, conv_w, conv_b, w_down, norm_post_ffn):
    batch, seq, _ = x.shape
    depth = w_in.shape[0]
    assert seq % MIX_TOKENS == 0 and seq % min(TM_INPROJ, seq) == 0 and seq % min(TM_TAIL, seq) == 0
    xf = x.reshape(batch * seq, D_MODEL).astype(F32)
    cos, sin = _rotary_tables(seq)
    tables = _retention_tables(RET_CHUNK)
    bias = _bias_table(rel_bias)
    w_in, w_out_a, w_out_r, w_out, w_up, w_down = (
        w.astype(BF16) for w in (w_in, w_out_a, w_out_r, w_out, w_up, w_down))
    for l in range(depth):
        qa, ka, vt, qkr, vrt, gst, gates = _inproj(xf, norm_pre_mix[l].astype(F32), w_in, l,
                                                   ret_norm[l].astype(F32), cos, sin, seq)
        aot, rot = _mixer(qa, ka, vt, qkr, vrt, gst, bias, sinks[l], tables, batch, seq)
        xf = _tail(xf, aot, rot, gates, w_out_a, w_out_r, w_out, norm_post_mix[l], norm_pre_ffn[l],
                   w_up, conv_w[l], conv_b[l], w_down, norm_post_ffn[l], l, seq)
    return xf.reshape(batch, seq, D_MODEL).astype(x.dtype)
```

```python
import functools
import math

import numpy as np
import jax
import jax.numpy as jnp
from jax import lax
from jax.experimental import pallas as pl
from jax.experimental.pallas import tpu as pltpu

D_MODEL = 1024
A_HEADS = 8
A_KV_HEADS = 2
A_HEAD_DIM = 64
A_GROUP = A_HEADS // A_KV_HEADS
WINDOW = 128
BLOCK = 128
NUM_BUCKETS = 32
MAX_DISTANCE = 128
R_HEADS = 4
R_QK_DIM = 128
R_V_DIM = 256
ROPE_BASE = 10000.0
D_FF = 2816
CONV_WIDTH = 3
EPS = 1e-6
NEG_INF = -1e30
LOG2E = math.log2(math.e)

A_Q = A_HEADS * A_HEAD_DIM
A_KV = A_KV_HEADS * A_HEAD_DIM
R_QK = R_HEADS * R_QK_DIM
R_V = R_HEADS * R_V_DIM
OFF_AK = A_Q
OFF_AV = OFF_AK + A_KV
OFF_RQK = OFF_AV + A_KV
OFF_RV = OFF_RQK + 2 * R_QK
OFF_RG = OFF_RV + R_V
OFF_GATES = OFF_RG + R_V
D_IN = OFF_GATES + 2 * D_MODEL
V7X_VMEM_LIMIT_BYTES = 56 * 1024 * 1024
SUBLANES = 8
LANES = 128
FF_CHUNK = 256
TM_INPROJ = 512
INPROJ_SPLIT = 2
TM_TAIL = 512
TAIL_SPLIT = 2
MIX_TOKENS = 8 * BLOCK
LAG_A = 2
LAG_B = 1
STAGE_SLOTS = 4
RET_CHUNK = 2 * BLOCK

BF16 = jnp.bfloat16
F32 = jnp.float32


def _resident(shape):
    nd = len(shape)
    return pl.BlockSpec(shape, lambda *_: (0,) * nd, pipeline_mode=pl.Buffered(1))


def _layer_resident(shape, layer):
    nd = len(shape)
    return pl.BlockSpec((None,) + tuple(shape), lambda *_: (layer,) + (0,) * nd, pipeline_mode=pl.Buffered(1))


def _rms(v, gain):
    return v * lax.rsqrt(jnp.mean(v * v, axis=-1, keepdims=True) + EPS) * gain


def _t5_band_buckets():
    i = np.arange(BLOCK)[:, None]
    j = np.arange(2 * BLOCK)[None, :]
    dist = BLOCK + i - j
    n = np.maximum(dist, 0)
    max_exact = NUM_BUCKETS // 2
    large = max_exact + (np.log(np.maximum(n, 1) / max_exact) / math.log(MAX_DISTANCE / max_exact)
                         * (NUM_BUCKETS - max_exact)).astype(np.int32)
    large = np.minimum(large, NUM_BUCKETS - 1)
    bucket = np.where(n < max_exact, n, large).astype(np.int32)
    valid = (dist >= 0) & (dist < WINDOW)
    return bucket, valid


def _rotary_tables(seq):
    half = R_QK_DIM // 2
    freqs = ROPE_BASE ** (-jnp.arange(half, dtype=F32) / half)
    ang = jnp.arange(seq).astype(F32)[:, None] * freqs[None, :]
    cos, sin = jnp.cos(ang), jnp.sin(ang)
    return jnp.concatenate([cos, cos], axis=-1), jnp.concatenate([-sin, sin], axis=-1)


def _retention_tables(chunk):
    log_gamma = jnp.log(1.0 - 2.0 ** (-5.0 - jnp.arange(R_HEADS, dtype=F32)))
    idx = jnp.arange(chunk, dtype=F32)
    rel = idx[None, :] - idx[:, None]
    decay_t = jnp.where(rel >= 0, jnp.exp(log_gamma[:, None, None] * jnp.maximum(rel, 0.0)), 0.0)
    zeta = jnp.exp(log_gamma[:, None] * (chunk - 1 - idx)[None, :])
    xi = jnp.exp(log_gamma[:, None] * (idx + 1.0)[None, :])
    gamma_c = jnp.exp(log_gamma * chunk)
    zeta_b = jnp.broadcast_to(zeta[:, :, None], (R_HEADS, chunk, R_QK_DIM))
    xi_b = jnp.broadcast_to(xi[:, :, None], (R_HEADS, chunk, R_QK_DIM))
    return decay_t, zeta_b, xi_b, gamma_c


def _bias_kernel(tab_ref, bucket_ref, valid_ref, out_ref):
    bucket = bucket_ref[...]
    valid = valid_ref[...] > 0
    in_cur = lax.broadcasted_iota(jnp.int32, (2 * BLOCK, BLOCK), 0) >= BLOCK
    for h in range(A_HEADS):
        kvh, g = divmod(h, A_GROUP)
        acc = jnp.zeros((2 * BLOCK, BLOCK), F32)
        for b in range(NUM_BUCKETS):
            acc = jnp.where(bucket == b, tab_ref[b, h] * LOG2E, acc)
        out_ref[1, kvh, :, g * BLOCK:(g + 1) * BLOCK] = jnp.where(valid, acc, NEG_INF)
        out_ref[0, kvh, :, g * BLOCK:(g + 1) * BLOCK] = jnp.where(valid & in_cur, acc, NEG_INF)


def _bias_table(rel_bias):
    bucket, valid = _t5_band_buckets()
    return pl.pallas_call(
        _bias_kernel,
        out_shape=jax.ShapeDtypeStruct((2, A_KV_HEADS, 2 * BLOCK, A_GROUP * BLOCK), F32),
        in_specs=[pl.BlockSpec(memory_space=pltpu.SMEM),
                  pl.BlockSpec(memory_space=pltpu.VMEM),
                  pl.BlockSpec(memory_space=pltpu.VMEM)],
        out_specs=pl.BlockSpec(memory_space=pltpu.VMEM),
        name="bias_table",
    )(rel_bias.astype(F32), jnp.asarray(bucket.T), jnp.asarray(valid.T.astype(np.int32)))


def _inproj_kernel(x_ref, g_ref, w_ref, rn_ref, cos_ref, sin_ref,
                   qa_ref, ka_ref, vt_ref, qkr_ref, vrt_ref, gst_ref, gates_ref):
    tm = x_ref.shape[0]
    hm = tm // INPROJ_SPLIT
    groups = [slice(i * hm, (i + 1) * hm) for i in range(INPROJ_SPLIT)]
    hs = [_rms(x_ref[rs, :], g_ref[...]).astype(BF16) for rs in groups]
    ones = jnp.ones((A_HEAD_DIM, hm), BF16)

    for rs, h in zip(groups, hs):
        def proj(lo, hi):
            return jnp.dot(h, w_ref[:, lo:hi], preferred_element_type=F32)

        qa_ref[rs, :] = (proj(0, OFF_AK) * (A_HEAD_DIM ** -0.5 * LOG2E)).astype(BF16)
        ka_ref[rs, :] = proj(OFF_AK, OFF_AV).astype(BF16)
        vat = proj(OFF_AV, OFF_RQK).astype(BF16).T
        for kvh in range(A_KV_HEADS):
            vt_ref[kvh * LANES:kvh * LANES + A_HEAD_DIM, rs] = vat[kvh * A_HEAD_DIM:(kvh + 1) * A_HEAD_DIM, :]
            vt_ref[kvh * LANES + A_HEAD_DIM:(kvh + 1) * LANES, rs] = ones
        vrt_ref[:, rs] = proj(OFF_RV, OFF_RG).astype(BF16).T
        g = proj(OFF_RG, OFF_GATES)
        gst_ref[:, rs] = (g * jax.nn.sigmoid(g) * rn_ref[...]).astype(BF16).T
        qk = proj(OFF_RQK, OFF_RV)
        cos = cos_ref[rs, :]
        sin = sin_ref[rs, :]
        for j in range(2 * R_HEADS):
            xh = qk[:, j * R_QK_DIM:(j + 1) * R_QK_DIM]
            rot = xh * cos + pltpu.roll(xh, R_QK_DIM // 2, 1) * sin
            if j >= R_HEADS:
                rot = rot * (R_QK_DIM ** -0.5)
            qkr_ref[rs, j * R_QK_DIM:(j + 1) * R_QK_DIM] = rot.astype(BF16)
        gates_ref[rs, :] = proj(OFF_GATES, D_IN).astype(BF16)


def _inproj(xf, gain, w, layer, ret_norm, cos, sin, seq):
    t = xf.shape[0]
    tm = min(TM_INPROJ, seq)
    tiles_per_seq = seq // tm
    row = lambda n: pl.BlockSpec((tm, n), lambda i: (i, 0))
    col = lambda n: pl.BlockSpec((n, tm), lambda i: (0, i))
    pos = pl.BlockSpec((tm, R_QK_DIM), lambda i: (i % tiles_per_seq, 0))
    tok = lambda n: jax.ShapeDtypeStruct((t, n), BF16)
    feat = lambda n: jax.ShapeDtypeStruct((n, t), BF16)
    return pl.pallas_call(
        _inproj_kernel,
        grid=(t // tm,),
        in_specs=[row(D_MODEL), _resident((1, D_MODEL)), _layer_resident((D_MODEL, D_IN), layer),
                  _resident((1, R_V)), pos, pos],
        out_specs=[row(A_Q), row(A_KV), col(2 * LANES), row(2 * R_QK), col(R_V), col(R_V), row(2 * D_MODEL)],
        out_shape=[tok(A_Q), tok(A_KV), feat(2 * LANES), tok(2 * R_QK), feat(R_V), feat(R_V), tok(2 * D_MODEL)],
        compiler_params=pltpu.CompilerParams(dimension_semantics=("parallel",),
                                             vmem_limit_bytes=V7X_VMEM_LIMIT_BYTES),
        name="inproj",
    )(xf, gain.reshape(1, D_MODEL), w, ret_norm.reshape(1, R_V), cos, sin)


def _mixer_kernel(sink_ref, gam_ref, qa_ref, kc_ref, kp_ref, vt_ref, vtp_ref, qkr_ref, vrt_ref, gst_ref,
                  bias_ref, dec_ref, zeta_ref, xi_ref, aot_ref, rot_ref, state_ref, s_scr, o_scr, a_scr, y_scr):
    step = pl.program_id(1)

    @pl.when(step == 0)
    def _():
        state_ref[...] = jnp.zeros_like(state_ref)

    contract_last = (((1,), (1,)), ((), ()))

    lane_head = lax.broadcasted_iota(jnp.int32, (1, A_GROUP * BLOCK), 1) // BLOCK
    sink_rows = []
    for kvh in range(A_KV_HEADS):
        r = jnp.full((1, A_GROUP * BLOCK), sink_ref[kvh * A_GROUP] * LOG2E, F32)
        for g in range(1, A_GROUP):
            r = jnp.where(lane_head == g, sink_ref[kvh * A_GROUP + g] * LOG2E, r)
        sink_rows.append(r)

    def att_first(c, kvh):
        cur = slice(c * BLOCK, (c + 1) * BLOCK)
        prv = slice((c - 1) * BLOCK, c * BLOCK)
        ks = slice(kvh * A_HEAD_DIM, (kvh + 1) * A_HEAD_DIM)
        k_prev = kp_ref[:, ks] if c == 0 else kc_ref[prv, ks]
        k_band = jnp.concatenate([k_prev, kc_ref[cur, ks]], axis=0)
        q = jnp.concatenate([qa_ref[cur, (kvh * A_GROUP + g) * A_HEAD_DIM:(kvh * A_GROUP + g + 1) * A_HEAD_DIM]
                             for g in range(A_GROUP)], axis=0)
        slot = (c * A_KV_HEADS + kvh) % STAGE_SLOTS
        s_scr[slot] = lax.dot_general(k_band, q, contract_last, preferred_element_type=F32)
        return slot

    def att_second(c, kvh, slot):
        cur = slice(c * BLOCK, (c + 1) * BLOCK)
        prv = slice((c - 1) * BLOCK, c * BLOCK)
        vs = slice(kvh * LANES, (kvh + 1) * LANES)
        variant = jnp.minimum(step, 1) if c == 0 else 1
        v_prev = vtp_ref[vs, :] if c == 0 else vt_ref[vs, prv]
        v_band = jnp.concatenate([v_prev, vt_ref[vs, cur]], axis=1)
        s = s_scr[slot] + bias_ref[variant, kvh]
        sink = sink_rows[kvh]
        m = jnp.maximum(jnp.max(s, axis=0, keepdims=True), sink)
        p = jnp.exp2(s - m).astype(BF16)
        o_scr[slot] = jnp.dot(v_band, p, preferred_element_type=F32)
        return slot, jnp.exp2(sink - m)

    def att_third(c, kvh, second):
        slot, esink = second
        o = o_scr[slot]
        cur = slice(c * BLOCK, (c + 1) * BLOCK)
        inv = 1.0 / (o[A_HEAD_DIM:A_HEAD_DIM + 1, :] + esink)
        res = (o[:A_HEAD_DIM, :] * inv).astype(BF16)
        for g in range(A_GROUP):
            hd = kvh * A_GROUP + g
            aot_ref[hd * A_HEAD_DIM:(hd + 1) * A_HEAD_DIM, cur] = res[:, g * BLOCK:(g + 1) * BLOCK]

    states = [state_ref[hd] for hd in range(R_HEADS)]

    def ret_first(c, hd):
        tok = slice(c * RET_CHUNK, (c + 1) * RET_CHUNK)
        q = qkr_ref[tok, hd * R_QK_DIM:(hd + 1) * R_QK_DIM]
        k = qkr_ref[tok, R_QK + hd * R_QK_DIM:R_QK + (hd + 1) * R_QK_DIM]
        slot = (c * R_HEADS + hd) % STAGE_SLOTS
        a_scr[slot] = lax.dot_general(k, q, contract_last, preferred_element_type=F32)
        return slot

    def ret_second(c, hd, slot):
        tok = slice(c * RET_CHUNK, (c + 1) * RET_CHUNK)
        q = qkr_ref[tok, hd * R_QK_DIM:(hd + 1) * R_QK_DIM]
        k = qkr_ref[tok, R_QK + hd * R_QK_DIM:R_QK + (hd + 1) * R_QK_DIM]
        vt = vrt_ref[hd * R_V_DIM:(hd + 1) * R_V_DIM, tok]
        st = states[hd]
        a = (a_scr[slot] * dec_ref[hd]).astype(BF16)
        qx = (q.astype(F32) * xi_ref[hd]).astype(BF16)
        y_scr[slot] = (jnp.dot(vt, a, preferred_element_type=F32)
                       + lax.dot_general(st.astype(BF16), qx, contract_last, preferred_element_type=F32))
        kz = (k.astype(F32) * zeta_ref[hd]).astype(BF16)
        states[hd] = gam_ref[hd] * st + jnp.dot(vt, kz, preferred_element_type=F32)
        return slot

    def ret_third(c, hd, slot):
        y = y_scr[slot]
        tok = slice(c * RET_CHUNK, (c + 1) * RET_CHUNK)
        vsl = slice(hd * R_V_DIM, (hd + 1) * R_V_DIM)
        mu = jnp.mean(y, axis=0, keepdims=True)
        d = y - mu
        var = jnp.mean(d * d, axis=0, keepdims=True)
        yn = d * lax.rsqrt(var + EPS)
        rot_ref[vsl, tok] = (gst_ref[vsl, tok].astype(F32) * yn).astype(BF16)

    att = [(att_first, att_second, att_third, c, kvh)
           for c in range(MIX_TOKENS // BLOCK) for kvh in range(A_KV_HEADS)]
    ret = [(ret_first, ret_second, ret_third, c, hd) for c in range(MIX_TOKENS // RET_CHUNK) for hd in range(R_HEADS)]
    items = [it for pair in zip(att, ret) for it in pair]
    firsts, seconds = {}, {}
    for i in range(len(items) + LAG_A + LAG_B):
        if i < len(items):
            f1, _, _, c, j = items[i]
            firsts[i] = f1(c, j)
        k = i - LAG_A
        if 0 <= k < len(items):
            _, f2, _, c, j = items[k]
            seconds[k] = f2(c, j, firsts.pop(k))
        k = i - LAG_A - LAG_B
        if 0 <= k < len(items):
            _, _, f3, c, j = items[k]
            f3(c, j, seconds.pop(k))
    for hd in range(R_HEADS):
        state_ref[hd] = states[hd]


def _mixer(qa, ka, vt, qkr, vrt, gst, bias, sinks, tables, batch, seq):
    decay_t, zeta_b, xi_b, gamma_c = tables
    t = qa.shape[0]
    tm = MIX_TOKENS
    steps = seq // tm
    nb = tm // BLOCK
    row = lambda n: pl.BlockSpec((tm, n), lambda b, c: (b * steps + c, 0))
    col = lambda n: pl.BlockSpec((n, tm), lambda b, c: (0, b * steps + c))
    prev_block = lambda b, c: b * steps * nb + jnp.maximum(c * nb - 1, 0)
    prow = pl.BlockSpec((BLOCK, A_KV), lambda b, c: (prev_block(b, c), 0))
    pcol = pl.BlockSpec((2 * LANES, BLOCK), lambda b, c: (0, prev_block(b, c)))
    smem = pl.BlockSpec(memory_space=pltpu.SMEM)
    return pl.pallas_call(
        _mixer_kernel,
        grid=(batch, steps),
        in_specs=[smem, smem, row(A_Q), row(A_KV), prow, col(2 * LANES), pcol, row(2 * R_QK), col(R_V), col(R_V),
                  _resident((2, A_KV_HEADS, 2 * BLOCK, A_GROUP * BLOCK)),
                  _resident((R_HEADS, RET_CHUNK, RET_CHUNK)), _resident((R_HEADS, RET_CHUNK, R_QK_DIM)),
                  _resident((R_HEADS, RET_CHUNK, R_QK_DIM))],
        out_specs=[col(A_Q), col(R_V)],
        scratch_shapes=[pltpu.VMEM((R_HEADS, R_V_DIM, R_QK_DIM), F32),
                        pltpu.VMEM((STAGE_SLOTS, 2 * BLOCK, A_GROUP * BLOCK), F32),
                        pltpu.VMEM((STAGE_SLOTS, LANES, A_GROUP * BLOCK), F32),
                        pltpu.VMEM((STAGE_SLOTS, RET_CHUNK, RET_CHUNK), F32),
                        pltpu.VMEM((STAGE_SLOTS, R_V_DIM, RET_CHUNK), F32)],
        out_shape=[jax.ShapeDtypeStruct((A_Q, t), BF16), jax.ShapeDtypeStruct((R_V, t), BF16)],
        compiler_params=pltpu.CompilerParams(dimension_semantics=("parallel", "arbitrary"),
                                             vmem_limit_bytes=V7X_VMEM_LIMIT_BYTES),
        name="mixer",
    )(sinks.astype(F32), gamma_c, qa, ka, ka, vt, vt, qkr, vrt, gst, bias, decay_t, zeta_b, xi_b)


def _gelu_tanh(v):
    return 0.5 * v * (1.0 + jnp.tanh(math.sqrt(2.0 / math.pi) * (v + 0.044715 * (v * v * v))))


def _tail_kernel(tiles_per_seq, x_ref, aot_ref, rot_ref, gates_ref, woa_ref, wor_ref, wo_ref,
                 npm_ref, npf_ref, wup_ref, cw_ref, cb_ref, wdn_ref, nff_ref,
                 out_ref, carry_ref, hid_ref):
    tm = x_ref.shape[0]
    hm = tm // TAIL_SPLIT
    contract_first = (((0,), (0,)), ((), ()))

    @pl.when(pl.program_id(0) % tiles_per_seq == 0)
    def _():
        carry_ref[...] = jnp.zeros_like(carry_ref)

    def mix_out(rs):
        ya = lax.dot_general(aot_ref[:, rs], woa_ref[...], contract_first, preferred_element_type=F32)
        yr = lax.dot_general(rot_ref[:, rs], wor_ref[...], contract_first, preferred_element_type=F32)
        gate_a = jax.nn.sigmoid(gates_ref[rs, :D_MODEL].astype(F32))
        gate_r = jax.nn.sigmoid(gates_ref[rs, D_MODEL:].astype(F32))
        merged = (gate_a * ya + gate_r * yr).astype(BF16)
        x1 = x_ref[rs, :] + _rms(jnp.dot(merged, wo_ref[...], preferred_element_type=F32), npm_ref[...])
        return x1, _rms(x1, npf_ref[...]).astype(BF16)

    def ffn(rs, x1, h):
        row = lax.broadcasted_iota(jnp.int32, (hm, FF_CHUNK), 0)
        for c in range(D_FF // FF_CHUNK):
            cs = slice(c * FF_CHUNK, (c + 1) * FF_CHUNK)
            a = jnp.dot(h, wup_ref[:, cs], preferred_element_type=F32)
            val = jnp.dot(h, wup_ref[:, D_FF + c * FF_CHUNK:D_FF + (c + 1) * FF_CHUNK], preferred_element_type=F32)
            last = carry_ref[SUBLANES - 1:SUBLANES, cs]
            last2 = carry_ref[SUBLANES - 2:SUBLANES - 1, cs]
            a1 = jnp.where(row == 0, last, pltpu.roll(a, 1, 0))
            a2 = jnp.where(row == 0, last2, jnp.where(row == 1, last, pltpu.roll(a, 2, 0)))
            conv = cb_ref[:, cs] + cw_ref[0:1, cs] * a2 + cw_ref[1:2, cs] * a1 + cw_ref[2:3, cs] * a
            carry_ref[:, cs] = a[hm - SUBLANES:, :]
            hid_ref[rs, cs] = (_gelu_tanh(conv) * val).astype(BF16)
        y = jnp.dot(hid_ref[rs, :], wdn_ref[...], preferred_element_type=F32)
        out_ref[rs, :] = x1 + _rms(y, nff_ref[...])

    groups = [slice(i * hm, (i + 1) * hm) for i in range(TAIL_SPLIT)]
    mixed = [mix_out(rs) for rs in groups]
    for rs, (x1, h) in zip(groups, mixed):
        ffn(rs, x1, h)


def _tail(xf, aot, rot, gates, woa, wor, wo, npm, npf, wup, cw, cb, wdn, nff, layer, seq):
    t = xf.shape[0]
    tm = min(TM_TAIL, seq)
    row = lambda n: pl.BlockSpec((tm, n), lambda i: (i, 0))
    col = lambda n: pl.BlockSpec((n, tm), lambda i: (0, i))
    vec = lambda v: v.reshape(1, -1).astype(F32)
    return pl.pallas_call(
        functools.partial(_tail_kernel, seq // tm),
        grid=(t // tm,),
        in_specs=[row(D_MODEL), col(A_Q), col(R_V), row(2 * D_MODEL),
                  _layer_resident((A_Q, D_MODEL), layer), _layer_resident((R_V, D_MODEL), layer),
                  _layer_resident((D_MODEL, D_MODEL), layer),
                  _resident((1, D_MODEL)), _resident((1, D_MODEL)),
                  _layer_resident((D_MODEL, 2 * D_FF), layer), _resident((CONV_WIDTH, D_FF)), _resident((1, D_FF)),
                  _layer_resident((D_FF, D_MODEL), layer), _resident((1, D_MODEL))],
        out_specs=row(D_MODEL),
        out_shape=jax.ShapeDtypeStruct((t, D_MODEL), F32),
        scratch_shapes=[pltpu.VMEM((SUBLANES, D_FF), F32), pltpu.VMEM((tm, D_FF), BF16)],
        compiler_params=pltpu.CompilerParams(dimension_semantics=("arbitrary",),
                                             vmem_limit_bytes=V7X_VMEM_LIMIT_BYTES),
        name="tail",
    )(xf, aot, rot, gates, woa, wor, wo, vec(npm), vec(npf), wup, cw.astype(F32), vec(cb), wdn, vec(nff))


def kernel(x, norm_pre_mix, w_in, sinks, rel_bias, ret_norm, w_out_a, w_out_r, w_out, norm_post_mix,
           norm_pre_ffn, w_up, conv_w, conv_b, w_down, norm_post_ffn):
    batch, seq, _ = x.shape
    depth = w_in.shape[0]
    assert seq % MIX_TOKENS == 0 and seq % min(TM_INPROJ, seq) == 0 and seq % min(TM_TAIL, seq) == 0
    xf = x.reshape(batch * seq, D_MODEL).astype(F32)
    cos, sin = _rotary_tables(seq)
    tables = _retention_tables(RET_CHUNK)
    bias = _bias_table(rel_bias)
    w_in, w_out_a, w_out_r, w_out, w_up, w_down = (
        w.astype(BF16) for w in (w_in, w_out_a, w_out_r, w_out, w_up, w_down))
    for l in range(depth):
        qa, ka, vt, qkr, vrt, gst, gates = _inproj(xf, norm_pre_mix[l].astype(F32), w_in, l,
                                                   ret_norm[l].astype(F32), cos, sin, seq)
        aot, rot = _mixer(qa, ka, vt, qkr, vrt, gst, bias, sinks[l], tables, batch, seq)
        xf = _tail(xf, aot, rot, gates, w_out_a, w_out_r, w_out, norm_post_mix[l], norm_pre_ffn[l],
                   w_up, conv_w[l], conv_b[l], w_down, norm_post_ffn[l], l, seq)
    return xf.reshape(batch, seq, D_MODEL).astype(x.dtype)
```

```python
import functools
import math

import numpy as np
import jax
import jax.numpy as jnp
from jax import lax
from jax.experimental import pallas as pl
from jax.experimental.pallas import tpu as pltpu

D_MODEL = 1024
A_HEADS = 8
A_KV_HEADS = 2
A_HEAD_DIM = 64
A_GROUP = A_HEADS // A_KV_HEADS
WINDOW = 128
BLOCK = 128
NUM_BUCKETS = 32
MAX_DISTANCE = 128
R_HEADS = 4
R_QK_DIM = 128
R_V_DIM = 256
ROPE_BASE = 10000.0
D_FF = 2816
CONV_WIDTH = 3
EPS = 1e-6
NEG_INF = -1e30
LOG2E = math.log2(math.e)

A_Q = A_HEADS * A_HEAD_DIM
A_KV = A_KV_HEADS * A_HEAD_DIM
R_QK = R_HEADS * R_QK_DIM
R_V = R_HEADS * R_V_DIM
OFF_AK = A_Q
OFF_AV = OFF_AK + A_KV
OFF_RQK = OFF_AV + A_KV
OFF_RV = OFF_RQK + 2 * R_QK
OFF_RG = OFF_RV + R_V
OFF_GATES = OFF_RG + R_V
D_IN = OFF_GATES + 2 * D_MODEL
V7X_VMEM_LIMIT_BYTES = 56 * 1024 * 1024
SUBLANES = 8
LANES = 128
FF_CHUNK = 256
TM_INPROJ = 1024
INPROJ_SPLIT = 4
TM_TAIL = 512
FM_TILE = TM_TAIL
TAIL_SPLIT = 2
MIX_TOKENS = 8 * BLOCK
LAG_A = 2
LAG_B = 1
STAGE_SLOTS = 4
RET_CHUNK = 2 * BLOCK

BF16 = jnp.bfloat16
F32 = jnp.float32


def _resident(shape):
    nd = len(shape)
    return pl.BlockSpec(shape, lambda *_: (0,) * nd, pipeline_mode=pl.Buffered(1))


def _layer_resident(shape, layer):
    nd = len(shape)
    return pl.BlockSpec((None,) + tuple(shape), lambda *_: (layer,) + (0,) * nd, pipeline_mode=pl.Buffered(1))


def _rms(v, gain):
    return v * lax.rsqrt(jnp.mean(v * v, axis=-1, keepdims=True) + EPS) * gain


def _t5_band_buckets():
    i = np.arange(BLOCK)[:, None]
    j = np.arange(2 * BLOCK)[None, :]
    dist = BLOCK + i - j
    n = np.maximum(dist, 0)
    max_exact = NUM_BUCKETS // 2
    large = max_exact + (np.log(np.maximum(n, 1) / max_exact) / math.log(MAX_DISTANCE / max_exact)
                         * (NUM_BUCKETS - max_exact)).astype(np.int32)
    large = np.minimum(large, NUM_BUCKETS - 1)
    bucket = np.where(n < max_exact, n, large).astype(np.int32)
    valid = (dist >= 0) & (dist < WINDOW)
    return bucket, valid


def _rotary_tables(seq):
    half = R_QK_DIM // 2
    freqs = ROPE_BASE ** (-jnp.arange(half, dtype=F32) / half)
    ang = jnp.arange(seq).astype(F32)[:, None] * freqs[None, :]
    cos, sin = jnp.cos(ang), jnp.sin(ang)
    return jnp.concatenate([cos, cos], axis=-1), jnp.concatenate([-sin, sin], axis=-1)


def _retention_tables(chunk):
    log_gamma = jnp.log(1.0 - 2.0 ** (-5.0 - jnp.arange(R_HEADS, dtype=F32)))
    idx = jnp.arange(chunk, dtype=F32)
    rel = idx[None, :] - idx[:, None]
    decay_t = jnp.where(rel >= 0, jnp.exp(log_gamma[:, None, None] * jnp.maximum(rel, 0.0)), 0.0)
    zeta = jnp.exp(log_gamma[:, None] * (chunk - 1 - idx)[None, :])
    xi = jnp.exp(log_gamma[:, None] * (idx + 1.0)[None, :])
    gamma_c = jnp.exp(log_gamma * chunk)
    zeta_b = jnp.broadcast_to(zeta[:, :, None], (R_HEADS, chunk, R_QK_DIM))
    xi_b = jnp.broadcast_to(xi[:, :, None], (R_HEADS, chunk, R_QK_DIM))
    return decay_t, zeta_b, xi_b, gamma_c


def _bias_kernel(tab_ref, bucket_ref, valid_ref, out_ref):
    bucket = bucket_ref[...]
    valid = valid_ref[...] > 0
    in_cur = lax.broadcasted_iota(jnp.int32, (2 * BLOCK, BLOCK), 0) >= BLOCK
    for h in range(A_HEADS):
        kvh, g = divmod(h, A_GROUP)
        acc = jnp.zeros((2 * BLOCK, BLOCK), F32)
        for b in range(NUM_BUCKETS):
            acc = jnp.where(bucket == b, tab_ref[b, h] * LOG2E, acc)
        out_ref[1, kvh, :, g * BLOCK:(g + 1) * BLOCK] = jnp.where(valid, acc, NEG_INF)
        out_ref[0, kvh, :, g * BLOCK:(g + 1) * BLOCK] = jnp.where(valid & in_cur, acc, NEG_INF)


def _bias_table(rel_bias):
    bucket, valid = _t5_band_buckets()
    return pl.pallas_call(
        _bias_kernel,
        out_shape=jax.ShapeDtypeStruct((2, A_KV_HEADS, 2 * BLOCK, A_GROUP * BLOCK), F32),
        in_specs=[pl.BlockSpec(memory_space=pltpu.SMEM),
                  pl.BlockSpec(memory_space=pltpu.VMEM),
                  pl.BlockSpec(memory_space=pltpu.VMEM)],
        out_specs=pl.BlockSpec(memory_space=pltpu.VMEM),
        name="bias_table",
    )(rel_bias.astype(F32), jnp.asarray(bucket.T), jnp.asarray(valid.T.astype(np.int32)))


def _inproj_kernel(x_ref, g_ref, w_ref, rn_ref, cos_ref, sin_ref,
                   qa_ref, ka_ref, vt_ref, qkr_ref, vrt_ref, gst_ref, gates_ref):
    tm = x_ref.shape[0]
    hm = tm // INPROJ_SPLIT
    groups = [slice(i * hm, (i + 1) * hm) for i in range(INPROJ_SPLIT)]
    hs = [_rms(x_ref[rs, :], g_ref[...]).astype(BF16) for rs in groups]
    ones = jnp.ones((A_HEAD_DIM, hm), BF16)

    for rs, h in zip(groups, hs):
        ft = rs.start // FM_TILE
        fl = slice(rs.start % FM_TILE, rs.start % FM_TILE + hm)

        def proj(lo, hi):
            return jnp.dot(h, w_ref[:, lo:hi], preferred_element_type=F32)

        qa_ref[rs, :] = (proj(0, OFF_AK) * (A_HEAD_DIM ** -0.5 * LOG2E)).astype(BF16)
        ka_ref[rs, :] = proj(OFF_AK, OFF_AV).astype(BF16)
        vat = proj(OFF_AV, OFF_RQK).astype(BF16).T
        for kvh in range(A_KV_HEADS):
            vt_ref[ft, kvh * LANES:kvh * LANES + A_HEAD_DIM, fl] = vat[kvh * A_HEAD_DIM:(kvh + 1) * A_HEAD_DIM, :]
            vt_ref[ft, kvh * LANES + A_HEAD_DIM:(kvh + 1) * LANES, fl] = ones
        vrt_ref[ft, :, fl] = proj(OFF_RV, OFF_RG).astype(BF16).T
        g = proj(OFF_RG, OFF_GATES)
        gst_ref[ft, :, fl] = (g * jax.nn.sigmoid(g) * rn_ref[...]).astype(BF16).T
        qk = proj(OFF_RQK, OFF_RV)
        cos = cos_ref[rs, :]
        sin = sin_ref[rs, :]
        for j in range(2 * R_HEADS):
            xh = qk[:, j * R_QK_DIM:(j + 1) * R_QK_DIM]
            rot = xh * cos + pltpu.roll(xh, R_QK_DIM // 2, 1) * sin
            if j >= R_HEADS:
                rot = rot * (R_QK_DIM ** -0.5)
            qkr_ref[rs, j * R_QK_DIM:(j + 1) * R_QK_DIM] = rot.astype(BF16)
        gates_ref[rs, :] = proj(OFF_GATES, D_IN).astype(BF16)


def _inproj(xf, gain, w, layer, ret_norm, cos, sin, seq):
    t = xf.shape[0]
    tm = TM_INPROJ
    tiles_per_seq = seq // tm
    row = lambda n: pl.BlockSpec((tm, n), lambda i: (i, 0))
    col = lambda n: pl.BlockSpec((tm // FM_TILE, n, FM_TILE), lambda i: (i, 0, 0))
    pos = pl.BlockSpec((tm, R_QK_DIM), lambda i: (i % tiles_per_seq, 0))
    tok = lambda n: jax.ShapeDtypeStruct((t, n), BF16)
    feat = lambda n: jax.ShapeDtypeStruct((t // FM_TILE, n, FM_TILE), BF16)
    return pl.pallas_call(
        _inproj_kernel,
        grid=(t // tm,),
        in_specs=[row(D_MODEL), _resident((1, D_MODEL)), _layer_resident((D_MODEL, D_IN), layer),
                  _resident((1, R_V)), pos, pos],
        out_specs=[row(A_Q), row(A_KV), col(2 * LANES), row(2 * R_QK), col(R_V), col(R_V), row(2 * D_MODEL)],
        out_shape=[tok(A_Q), tok(A_KV), feat(2 * LANES), tok(2 * R_QK), feat(R_V), feat(R_V), tok(2 * D_MODEL)],
        compiler_params=pltpu.CompilerParams(dimension_semantics=("parallel",),
                                             vmem_limit_bytes=V7X_VMEM_LIMIT_BYTES),
        name="inproj",
    )(xf, gain.reshape(1, D_MODEL), w, ret_norm.reshape(1, R_V), cos, sin)


def _fm(i, width):
    start = i * width
    return start // FM_TILE, slice(start % FM_TILE, start % FM_TILE + width)


def _mixer_kernel(sink_ref, gam_ref, qa_ref, kc_ref, kp_ref, vt_ref, vtp_ref, qkr_ref, vrt_ref, gst_ref,
                  bias_ref, dec_ref, zeta_ref, xi_ref, aot_ref, rot_ref, state_ref, s_scr, o_scr, a_scr, y_scr):
    step = pl.program_id(1)

    @pl.when(step == 0)
    def _():
        state_ref[...] = jnp.zeros_like(state_ref)

    contract_last = (((1,), (1,)), ((), ()))

    lane_head = lax.broadcasted_iota(jnp.int32, (1, A_GROUP * BLOCK), 1) // BLOCK
    sink_rows = []
    for kvh in range(A_KV_HEADS):
        r = jnp.full((1, A_GROUP * BLOCK), sink_ref[kvh * A_GROUP] * LOG2E, F32)
        for g in range(1, A_GROUP):
            r = jnp.where(lane_head == g, sink_ref[kvh * A_GROUP + g] * LOG2E, r)
        sink_rows.append(r)

    def att_first(c, kvh):
        cur = slice(c * BLOCK, (c + 1) * BLOCK)
        prv = slice((c - 1) * BLOCK, c * BLOCK)
        ks = slice(kvh * A_HEAD_DIM, (kvh + 1) * A_HEAD_DIM)
        k_prev = kp_ref[:, ks] if c == 0 else kc_ref[prv, ks]
        k_band = jnp.concatenate([k_prev, kc_ref[cur, ks]], axis=0)
        q = jnp.concatenate([qa_ref[cur, (kvh * A_GROUP + g) * A_HEAD_DIM:(kvh * A_GROUP + g + 1) * A_HEAD_DIM]
                             for g in range(A_GROUP)], axis=0)
        slot = (c * A_KV_HEADS + kvh) % STAGE_SLOTS
        s_scr[slot] = lax.dot_general(k_band, q, contract_last, preferred_element_type=F32)
        return slot

    def att_second(c, kvh, slot):
        vs = slice(kvh * LANES, (kvh + 1) * LANES)
        variant = jnp.minimum(step, 1) if c == 0 else 1
        ft, fl = _fm(c, BLOCK)
        pt, pls = _fm(c - 1, BLOCK)
        v_prev = vtp_ref[vs, :] if c == 0 else vt_ref[pt, vs, pls]
        v_band = jnp.concatenate([v_prev, vt_ref[ft, vs, fl]], axis=1)
        s = s_scr[slot] + bias_ref[variant, kvh]
        sink = sink_rows[kvh]
        m = jnp.maximum(jnp.max(s, axis=0, keepdims=True), sink)
        p = jnp.exp2(s - m).astype(BF16)
        o_scr[slot] = jnp.dot(v_band, p, preferred_element_type=F32)
        return slot, jnp.exp2(sink - m)

    def att_third(c, kvh, second):
        slot, esink = second
        o = o_scr[slot]
        ft, fl = _fm(c, BLOCK)
        inv = 1.0 / (o[A_HEAD_DIM:A_HEAD_DIM + 1, :] + esink)
        res = (o[:A_HEAD_DIM, :] * inv).astype(BF16)
        for g in range(A_GROUP):
            hd = kvh * A_GROUP + g
            aot_ref[ft, hd * A_HEAD_DIM:(hd + 1) * A_HEAD_DIM, fl] = res[:, g * BLOCK:(g + 1) * BLOCK]

    states = [state_ref[hd] for hd in range(R_HEADS)]

    def ret_first(c, hd):
        tok = slice(c * RET_CHUNK, (c + 1) * RET_CHUNK)
        q = qkr_ref[tok, hd * R_QK_DIM:(hd + 1) * R_QK_DIM]
        k = qkr_ref[tok, R_QK + hd * R_QK_DIM:R_QK + (hd + 1) * R_QK_DIM]
        slot = (c * R_HEADS + hd) % STAGE_SLOTS
        a_scr[slot] = lax.dot_general(k, q, contract_last, preferred_element_type=F32)
        return slot

    def ret_second(c, hd, slot):
        tok = slice(c * RET_CHUNK, (c + 1) * RET_CHUNK)
        q = qkr_ref[tok, hd * R_QK_DIM:(hd + 1) * R_QK_DIM]
        k = qkr_ref[tok, R_QK + hd * R_QK_DIM:R_QK + (hd + 1) * R_QK_DIM]
        ft, fl = _fm(c, RET_CHUNK)
        vt = vrt_ref[ft, hd * R_V_DIM:(hd + 1) * R_V_DIM, fl]
        st = states[hd]
        a = (a_scr[slot] * dec_ref[hd]).astype(BF16)
        qx = (q.astype(F32) * xi_ref[hd]).astype(BF16)
        y_scr[slot] = (jnp.dot(vt, a, preferred_element_type=F32)
                       + lax.dot_general(st.astype(BF16), qx, contract_last, preferred_element_type=F32))
        kz = (k.astype(F32) * zeta_ref[hd]).astype(BF16)
        states[hd] = gam_ref[hd] * st + jnp.dot(vt, kz, preferred_element_type=F32)
        return slot

    def ret_third(c, hd, slot):
        y = y_scr[slot]
        vsl = slice(hd * R_V_DIM, (hd + 1) * R_V_DIM)
        mu = jnp.mean(y, axis=0, keepdims=True)
        d = y - mu
        var = jnp.mean(d * d, axis=0, keepdims=True)
        yn = d * lax.rsqrt(var + EPS)
        ft, fl = _fm(c, RET_CHUNK)
        rot_ref[ft, vsl, fl] = (gst_ref[ft, vsl, fl].astype(F32) * yn).astype(BF16)

    att = [(att_first, att_second, att_third, c, kvh)
           for c in range(MIX_TOKENS // BLOCK) for kvh in range(A_KV_HEADS)]
    ret = [(ret_first, ret_second, ret_third, c, hd) for c in range(MIX_TOKENS // RET_CHUNK) for hd in range(R_HEADS)]
    items = [it for pair in zip(att, ret) for it in pair]
    firsts, seconds = {}, {}
    for i in range(len(items) + LAG_A + LAG_B):
        if i < len(items):
            f1, _, _, c, j = items[i]
            firsts[i] = f1(c, j)
        k = i - LAG_A
        if 0 <= k < len(items):
            _, f2, _, c, j = items[k]
            seconds[k] = f2(c, j, firsts.pop(k))
        k = i - LAG_A - LAG_B
        if 0 <= k < len(items):
            _, _, f3, c, j = items[k]
            f3(c, j, seconds.pop(k))
    for hd in range(R_HEADS):
        state_ref[hd] = states[hd]


def _mixer(qa, ka, vt, qkr, vrt, gst, bias, sinks, tables, batch, seq):
    decay_t, zeta_b, xi_b, gamma_c = tables
    t = qa.shape[0]
    tm = MIX_TOKENS
    steps = seq // tm
    nb = tm // BLOCK
    row = lambda n: pl.BlockSpec((tm, n), lambda b, c: (b * steps + c, 0))
    col = lambda n: pl.BlockSpec((tm // FM_TILE, n, FM_TILE), lambda b, c: (b * steps + c, 0, 0))
    prev_block = lambda b, c: b * steps * nb + jnp.maximum(c * nb - 1, 0)
    prow = pl.BlockSpec((BLOCK, A_KV), lambda b, c: (prev_block(b, c), 0))
    per_tile = FM_TILE // BLOCK
    pcol = pl.BlockSpec((None, 2 * LANES, BLOCK),
                        lambda b, c: (prev_block(b, c) // per_tile, 0, prev_block(b, c) % per_tile))
    smem = pl.BlockSpec(memory_space=pltpu.SMEM)
    return pl.pallas_call(
        _mixer_kernel,
        grid=(batch, steps),
        in_specs=[smem, smem, row(A_Q), row(A_KV), prow, col(2 * LANES), pcol, row(2 * R_QK), col(R_V), col(R_V),
                  _resident((2, A_KV_HEADS, 2 * BLOCK, A_GROUP * BLOCK)),
                  _resident((R_HEADS, RET_CHUNK, RET_CHUNK)), _resident((R_HEADS, RET_CHUNK, R_QK_DIM)),
                  _resident((R_HEADS, RET_CHUNK, R_QK_DIM))],
        out_specs=[col(A_Q), col(R_V)],
        scratch_shapes=[pltpu.VMEM((R_HEADS, R_V_DIM, R_QK_DIM), F32),
                        pltpu.VMEM((STAGE_SLOTS, 2 * BLOCK, A_GROUP * BLOCK), F32),
                        pltpu.VMEM((STAGE_SLOTS, LANES, A_GROUP * BLOCK), F32),
                        pltpu.VMEM((STAGE_SLOTS, RET_CHUNK, RET_CHUNK), F32),
                        pltpu.VMEM((STAGE_SLOTS, R_V_DIM, RET_CHUNK), F32)],
        out_shape=[jax.ShapeDtypeStruct((t // FM_TILE, A_Q, FM_TILE), BF16),
                   jax.ShapeDtypeStruct((t // FM_TILE, R_V, FM_TILE), BF16)],
        compiler_params=pltpu.CompilerParams(dimension_semantics=("parallel", "arbitrary"),
                                             vmem_limit_bytes=V7X_VMEM_LIMIT_BYTES),
        name="mixer",
    )(sinks.astype(F32), gamma_c, qa, ka, ka, vt, vt, qkr, vrt, gst, bias, decay_t, zeta_b, xi_b)


def _gelu_tanh(v):
    return 0.5 * v * (1.0 + jnp.tanh(math.sqrt(2.0 / math.pi) * (v + 0.044715 * (v * v * v))))


def _tail_kernel(tiles_per_seq, x_ref, aot_ref, rot_ref, gates_ref, woa_ref, wor_ref, wo_ref,
                 npm_ref, npf_ref, wup_ref, cw_ref, cb_ref, wdn_ref, nff_ref,
                 out_ref, carry_ref, hid_ref):
    tm = x_ref.shape[0]
    hm = tm // TAIL_SPLIT
    contract_first = (((0,), (0,)), ((), ()))

    @pl.when(pl.program_id(0) % tiles_per_seq == 0)
    def _():
        carry_ref[...] = jnp.zeros_like(carry_ref)

    def mix_out(rs):
        ya = lax.dot_general(aot_ref[:, rs], woa_ref[...], contract_first, preferred_element_type=F32)
        yr = lax.dot_general(rot_ref[:, rs], wor_ref[...], contract_first, preferred_element_type=F32)
        gate_a = jax.nn.sigmoid(gates_ref[rs, :D_MODEL].astype(F32))
        gate_r = jax.nn.sigmoid(gates_ref[rs, D_MODEL:].astype(F32))
        merged = (gate_a * ya + gate_r * yr).astype(BF16)
        x1 = x_ref[rs, :] + _rms(jnp.dot(merged, wo_ref[...], preferred_element_type=F32), npm_ref[...])
        return x1, _rms(x1, npf_ref[...]).astype(BF16)

    def ffn(rs, x1, h):
        row = lax.broadcasted_iota(jnp.int32, (hm, FF_CHUNK), 0)
        for c in range(D_FF // FF_CHUNK):
            cs = slice(c * FF_CHUNK, (c + 1) * FF_CHUNK)
            a = jnp.dot(h, wup_ref[:, cs], preferred_element_type=F32)
            val = jnp.dot(h, wup_ref[:, D_FF + c * FF_CHUNK:D_FF + (c + 1) * FF_CHUNK], preferred_element_type=F32)
            last = carry_ref[SUBLANES - 1:SUBLANES, cs]
            last2 = carry_ref[SUBLANES - 2:SUBLANES - 1, cs]
            a1 = jnp.where(row == 0, last, pltpu.roll(a, 1, 0))
            a2 = jnp.where(row == 0, last2, jnp.where(row == 1, last, pltpu.roll(a, 2, 0)))
            conv = cb_ref[:, cs] + cw_ref[0:1, cs] * a2 + cw_ref[1:2, cs] * a1 + cw_ref[2:3, cs] * a
            carry_ref[:, cs] = a[hm - SUBLANES:, :]
            hid_ref[rs, cs] = (_gelu_tanh(conv) * val).astype(BF16)
        y = jnp.dot(hid_ref[rs, :], wdn_ref[...], preferred_element_type=F32)
        out_ref[rs, :] = x1 + _rms(y, nff_ref[...])

    groups = [slice(i * hm, (i + 1) * hm) for i in range(TAIL_SPLIT)]
    mixed = [mix_out(rs) for rs in groups]
    for rs, (x1, h) in zip(groups, mixed):
        ffn(rs, x1, h)


def _tail(xf, aot, rot, gates, woa, wor, wo, npm, npf, wup, cw, cb, wdn, nff, layer, seq):
    t = xf.shape[0]
    tm = TM_TAIL
    row = lambda n: pl.BlockSpec((tm, n), lambda i: (i, 0))
    col = lambda n: pl.BlockSpec((None, n, FM_TILE), lambda i: (i, 0, 0))
    vec = lambda v: v.reshape(1, -1).astype(F32)
    return pl.pallas_call(
        functools.partial(_tail_kernel, seq // tm),
        grid=(t // tm,),
        in_specs=[row(D_MODEL), col(A_Q), col(R_V), row(2 * D_MODEL),
                  _layer_resident((A_Q, D_MODEL), layer), _layer_resident((R_V, D_MODEL), layer),
                  _layer_resident((D_MODEL, D_MODEL), layer),
                  _resident((1, D_MODEL)), _resident((1, D_MODEL)),
                  _layer_resident((D_MODEL, 2 * D_FF), layer), _resident((CONV_WIDTH, D_FF)), _resident((1, D_FF)),
                  _layer_resident((D_FF, D_MODEL), layer), _resident((1, D_MODEL))],
        out_specs=row(D_MODEL),
        out_shape=jax.ShapeDtypeStruct((t, D_MODEL), F32),
        scratch_shapes=[pltpu.VMEM((SUBLANES, D_FF), F32), pltpu.VMEM((tm, D_FF), BF16)],
        compiler_params=pltpu.CompilerParams(dimension_semantics=("arbitrary",),
                                             vmem_limit_bytes=V7X_VMEM_LIMIT_BYTES),
        name="tail",
    )(xf, aot, rot, gates, woa, wor, wo, vec(npm), vec(npf), wup, cw.astype(F32), vec(cb), wdn, vec(nff))


def kernel(x, norm_pre_mix, w_in, sinks, rel_bias, ret_norm, w_out_a, w_out_r, w_out, norm_post_mix,
           norm_pre_ffn, w_up, conv_w, conv_b, w_down, norm_post_ffn):
    batch, seq, _ = x.shape
    depth = w_in.shape[0]
    assert seq % MIX_TOKENS == 0 and seq % TM_INPROJ == 0 and seq % TM_TAIL == 0
    assert MIX_TOKENS % FM_TILE == 0 and TM_INPROJ % FM_TILE == 0 and FM_TILE % (TM_INPROJ // INPROJ_SPLIT) == 0
    xf = x.reshape(batch * seq, D_MODEL).astype(F32)
    cos, sin = _rotary_tables(seq)
    tables = _retention_tables(RET_CHUNK)
    bias = _bias_table(rel_bias)
    w_in, w_out_a, w_out_r, w_out, w_up, w_down = (
        w.astype(BF16) for w in (w_in, w_out_a, w_out_r, w_out, w_up, w_down))
    for l in range(depth):
        qa, ka, vt, qkr, vrt, gst, gates = _inproj(xf, norm_pre_mix[l].astype(F32), w_in, l,
                                                   ret_norm[l].astype(F32), cos, sin, seq)
        aot, rot = _mixer(qa, ka, vt, qkr, vrt, gst, bias, sinks[l], tables, batch, seq)
        xf = _tail(xf, aot, rot, gates, w_out_a, w_out_r, w_out, norm_post_mix[l], norm_pre_ffn[l],
                   w_up, conv_w[l], conv_b[l], w_down, norm_post_ffn[l], l, seq)
    return xf.reshape(batch, seq, D_MODEL).astype(x.dtype)
```

```python
import functools
import math

import numpy as np
import jax
import jax.numpy as jnp
from jax import lax
from jax.experimental import pallas as pl
from jax.experimental.pallas import tpu as pltpu

D_MODEL = 1024
A_HEADS = 8
A_KV_HEADS = 2
A_HEAD_DIM = 64
A_GROUP = A_HEADS // A_KV_HEADS
WINDOW = 128
BLOCK = 128
NUM_BUCKETS = 32
MAX_DISTANCE = 128
R_HEADS = 4
R_QK_DIM = 128
R_V_DIM = 256
ROPE_BASE = 10000.0
D_FF = 2816
CONV_WIDTH = 3
EPS = 1e-6
NEG_INF = -1e30
LOG2E = math.log2(math.e)

A_Q = A_HEADS * A_HEAD_DIM
A_KV = A_KV_HEADS * A_HEAD_DIM
R_QK = R_HEADS * R_QK_DIM
R_V = R_HEADS * R_V_DIM
OFF_AK = A_Q
OFF_AV = OFF_AK + A_KV
OFF_RQK = OFF_AV + A_KV
OFF_RV = OFF_RQK + 2 * R_QK
OFF_RG = OFF_RV + R_V
OFF_GATES = OFF_RG + R_V
D_IN = OFF_GATES + 2 * D_MODEL
V7X_VMEM_LIMIT_BYTES = 56 * 1024 * 1024
SUBLANES = 8
LANES = 128
FF_CHUNK = 256
TM_INPROJ = 1024
INPROJ_SPLIT = 4
TM_TAIL = 512
FM_TILE = TM_TAIL
TAIL_SPLIT = 2
MIX_TOKENS = 8 * BLOCK
LAG_A = 2
LAG_B = 1
STAGE_SLOTS = 4
RET_CHUNK = 2 * BLOCK

BF16 = jnp.bfloat16
F32 = jnp.float32


def _resident(shape):
    nd = len(shape)
    return pl.BlockSpec(shape, lambda *_: (0,) * nd, pipeline_mode=pl.Buffered(1))


def _layer_resident(shape, layer):
    nd = len(shape)
    return pl.BlockSpec((None,) + tuple(shape), lambda *_: (layer,) + (0,) * nd, pipeline_mode=pl.Buffered(1))


def _rms(v, gain):
    return v * lax.rsqrt(jnp.mean(v * v, axis=-1, keepdims=True) + EPS) * gain


def _t5_band_buckets():
    i = np.arange(BLOCK)[:, None]
    j = np.arange(2 * BLOCK)[None, :]
    dist = BLOCK + i - j
    n = np.maximum(dist, 0)
    max_exact = NUM_BUCKETS // 2
    large = max_exact + (np.log(np.maximum(n, 1) / max_exact) / math.log(MAX_DISTANCE / max_exact)
                         * (NUM_BUCKETS - max_exact)).astype(np.int32)
    large = np.minimum(large, NUM_BUCKETS - 1)
    bucket = np.where(n < max_exact, n, large).astype(np.int32)
    valid = (dist >= 0) & (dist < WINDOW)
    return bucket, valid


def _rotary_tables(seq):
    half = R_QK_DIM // 2
    freqs = ROPE_BASE ** (-jnp.arange(half, dtype=F32) / half)
    ang = jnp.arange(seq).astype(F32)[:, None] * freqs[None, :]
    cos, sin = jnp.cos(ang), jnp.sin(ang)
    return jnp.concatenate([cos, cos], axis=-1), jnp.concatenate([-sin, sin], axis=-1)


def _retention_tables(chunk):
    log_gamma = jnp.log(1.0 - 2.0 ** (-5.0 - jnp.arange(R_HEADS, dtype=F32)))
    idx = jnp.arange(chunk, dtype=F32)
    rel = idx[None, :] - idx[:, None]
    decay_t = jnp.where(rel >= 0, jnp.exp(log_gamma[:, None, None] * jnp.maximum(rel, 0.0)), 0.0)
    zeta = jnp.exp(log_gamma[:, None] * (chunk - 1 - idx)[None, :])
    xi = jnp.exp(log_gamma[:, None] * (idx + 1.0)[None, :])
    gamma_c = jnp.exp(log_gamma * chunk)
    zeta_b = jnp.broadcast_to(zeta[:, :, None], (R_HEADS, chunk, R_QK_DIM))
    xi_b = jnp.broadcast_to(xi[:, :, None], (R_HEADS, chunk, R_QK_DIM))
    return decay_t, zeta_b, xi_b, gamma_c


def _bias_kernel(tab_ref, bucket_ref, valid_ref, out_ref):
    bucket = bucket_ref[...]
    valid = valid_ref[...] > 0
    in_cur = lax.broadcasted_iota(jnp.int32, (2 * BLOCK, BLOCK), 0) >= BLOCK
    for h in range(A_HEADS):
        kvh, g = divmod(h, A_GROUP)
        acc = jnp.zeros((2 * BLOCK, BLOCK), F32)
        for b in range(NUM_BUCKETS):
            acc = jnp.where(bucket == b, tab_ref[b, h] * LOG2E, acc)
        out_ref[1, kvh, :, g * BLOCK:(g + 1) * BLOCK] = jnp.where(valid, acc, NEG_INF)
        out_ref[0, kvh, :, g * BLOCK:(g + 1) * BLOCK] = jnp.where(valid & in_cur, acc, NEG_INF)


def _bias_table(rel_bias):
    bucket, valid = _t5_band_buckets()
    return pl.pallas_call(
        _bias_kernel,
        out_shape=jax.ShapeDtypeStruct((2, A_KV_HEADS, 2 * BLOCK, A_GROUP * BLOCK), F32),
        in_specs=[pl.BlockSpec(memory_space=pltpu.SMEM),
                  pl.BlockSpec(memory_space=pltpu.VMEM),
                  pl.BlockSpec(memory_space=pltpu.VMEM)],
        out_specs=pl.BlockSpec(memory_space=pltpu.VMEM),
        name="bias_table",
    )(rel_bias.astype(F32), jnp.asarray(bucket.T), jnp.asarray(valid.T.astype(np.int32)))


def _inproj_kernel(x_ref, g_ref, w_ref, rn_ref, cos_ref, sin_ref,
                   qa_ref, ka_ref, vt_ref, qkr_ref, vrt_ref, gst_ref, gates_ref):
    tm = x_ref.shape[0]
    hm = tm // INPROJ_SPLIT
    groups = [slice(i * hm, (i + 1) * hm) for i in range(INPROJ_SPLIT)]
    hs = [_rms(x_ref[rs, :], g_ref[...]).astype(BF16) for rs in groups]
    ones = jnp.ones((A_HEAD_DIM, hm), BF16)

    for rs, h in zip(groups, hs):
        ft = rs.start // FM_TILE
        fl = slice(rs.start % FM_TILE, rs.start % FM_TILE + hm)

        def proj(lo, hi):
            return jnp.dot(h, w_ref[:, lo:hi], preferred_element_type=F32)

        qa_ref[rs, :] = (proj(0, OFF_AK) * (A_HEAD_DIM ** -0.5 * LOG2E)).astype(BF16)
        ka_ref[rs, :] = proj(OFF_AK, OFF_AV).astype(BF16)
        vat = proj(OFF_AV, OFF_RQK).astype(BF16).T
        for kvh in range(A_KV_HEADS):
            vt_ref[ft, kvh * LANES:kvh * LANES + A_HEAD_DIM, fl] = vat[kvh * A_HEAD_DIM:(kvh + 1) * A_HEAD_DIM, :]
            vt_ref[ft, kvh * LANES + A_HEAD_DIM:(kvh + 1) * LANES, fl] = ones
        vrt_ref[ft, :, fl] = proj(OFF_RV, OFF_RG).astype(BF16).T
        g = proj(OFF_RG, OFF_GATES)
        gst_ref[ft, :, fl] = (g * jax.nn.sigmoid(g) * rn_ref[...]).astype(BF16).T
        qk = proj(OFF_RQK, OFF_RV)
        cos = cos_ref[rs, :]
        sin = sin_ref[rs, :]
        for j in range(2 * R_HEADS):
            xh = qk[:, j * R_QK_DIM:(j + 1) * R_QK_DIM]
            rot = xh * cos + pltpu.roll(xh, R_QK_DIM // 2, 1) * sin
            if j >= R_HEADS:
                rot = rot * (R_QK_DIM ** -0.5)
            qkr_ref[rs, j * R_QK_DIM:(j + 1) * R_QK_DIM] = rot.astype(BF16)
        gates_ref[rs, :] = jax.nn.sigmoid(proj(OFF_GATES, D_IN)).astype(BF16)


def _inproj(xf, gain, w, layer, ret_norm, cos, sin, seq):
    t = xf.shape[0]
    tm = TM_INPROJ
    tiles_per_seq = seq // tm
    row = lambda n: pl.BlockSpec((tm, n), lambda i: (i, 0))
    col = lambda n: pl.BlockSpec((tm // FM_TILE, n, FM_TILE), lambda i: (i, 0, 0))
    pos = pl.BlockSpec((tm, R_QK_DIM), lambda i: (i % tiles_per_seq, 0))
    tok = lambda n: jax.ShapeDtypeStruct((t, n), BF16)
    feat = lambda n: jax.ShapeDtypeStruct((t // FM_TILE, n, FM_TILE), BF16)
    return pl.pallas_call(
        _inproj_kernel,
        grid=(t // tm,),
        in_specs=[row(D_MODEL), _resident((1, D_MODEL)), _layer_resident((D_MODEL, D_IN), layer),
                  _resident((1, R_V)), pos, pos],
        out_specs=[row(A_Q), row(A_KV), col(2 * LANES), row(2 * R_QK), col(R_V), col(R_V), row(2 * D_MODEL)],
        out_shape=[tok(A_Q), tok(A_KV), feat(2 * LANES), tok(2 * R_QK), feat(R_V), feat(R_V), tok(2 * D_MODEL)],
        compiler_params=pltpu.CompilerParams(dimension_semantics=("parallel",),
                                             vmem_limit_bytes=V7X_VMEM_LIMIT_BYTES),
        name="inproj",
    )(xf, gain.reshape(1, D_MODEL), w, ret_norm.reshape(1, R_V), cos, sin)


def _fm(i, width):
    start = i * width
    return start // FM_TILE, slice(start % FM_TILE, start % FM_TILE + width)


def _mixer_kernel(sink_ref, gam_ref, qa_ref, kc_ref, kp_ref, vt_ref, vtp_ref, qkr_ref, vrt_ref, gst_ref,
                  bias_ref, dec_ref, zeta_ref, xi_ref, aot_ref, rot_ref, state_ref, s_scr, o_scr, a_scr, y_scr):
    step = pl.program_id(1)

    @pl.when(step == 0)
    def _():
        state_ref[...] = jnp.zeros_like(state_ref)

    contract_last = (((1,), (1,)), ((), ()))

    lane_head = lax.broadcasted_iota(jnp.int32, (1, A_GROUP * BLOCK), 1) // BLOCK
    sink_rows = []
    for kvh in range(A_KV_HEADS):
        r = jnp.full((1, A_GROUP * BLOCK), sink_ref[kvh * A_GROUP] * LOG2E, F32)
        for g in range(1, A_GROUP):
            r = jnp.where(lane_head == g, sink_ref[kvh * A_GROUP + g] * LOG2E, r)
        sink_rows.append(r)

    def att_first(c, kvh):
        cur = slice(c * BLOCK, (c + 1) * BLOCK)
        prv = slice((c - 1) * BLOCK, c * BLOCK)
        ks = slice(kvh * A_HEAD_DIM, (kvh + 1) * A_HEAD_DIM)
        k_prev = kp_ref[:, ks] if c == 0 else kc_ref[prv, ks]
        k_band = jnp.concatenate([k_prev, kc_ref[cur, ks]], axis=0)
        q = jnp.concatenate([qa_ref[cur, (kvh * A_GROUP + g) * A_HEAD_DIM:(kvh * A_GROUP + g + 1) * A_HEAD_DIM]
                             for g in range(A_GROUP)], axis=0)
        slot = (c * A_KV_HEADS + kvh) % STAGE_SLOTS
        s_scr[slot] = lax.dot_general(k_band, q, contract_last, preferred_element_type=F32)
        return slot

    def att_second(c, kvh, slot):
        vs = slice(kvh * LANES, (kvh + 1) * LANES)
        variant = jnp.minimum(step, 1) if c == 0 else 1
        ft, fl = _fm(c, BLOCK)
        pt, pls = _fm(c - 1, BLOCK)
        v_prev = vtp_ref[vs, :] if c == 0 else vt_ref[pt, vs, pls]
        v_band = jnp.concatenate([v_prev, vt_ref[ft, vs, fl]], axis=1)
        s = s_scr[slot] + bias_ref[variant, kvh]
        sink = sink_rows[kvh]
        m = jnp.maximum(jnp.max(s, axis=0, keepdims=True), sink)
        p = jnp.exp2(s - m).astype(BF16)
        o_scr[slot] = jnp.dot(v_band, p, preferred_element_type=F32)
        return slot, jnp.exp2(sink - m)

    def att_third(c, kvh, second):
        slot, esink = second
        o = o_scr[slot]
        ft, fl = _fm(c, BLOCK)
        inv = 1.0 / (o[A_HEAD_DIM:A_HEAD_DIM + 1, :] + esink)
        res = (o[:A_HEAD_DIM, :] * inv).astype(BF16)
        for g in range(A_GROUP):
            hd = kvh * A_GROUP + g
            aot_ref[ft, hd * A_HEAD_DIM:(hd + 1) * A_HEAD_DIM, fl] = res[:, g * BLOCK:(g + 1) * BLOCK]

    states = [state_ref[hd] for hd in range(R_HEADS)]

    def ret_first(c, hd):
        tok = slice(c * RET_CHUNK, (c + 1) * RET_CHUNK)
        q = qkr_ref[tok, hd * R_QK_DIM:(hd + 1) * R_QK_DIM]
        k = qkr_ref[tok, R_QK + hd * R_QK_DIM:R_QK + (hd + 1) * R_QK_DIM]
        slot = (c * R_HEADS + hd) % STAGE_SLOTS
        a_scr[slot] = lax.dot_general(k, q, contract_last, preferred_element_type=F32)
        return slot

    def ret_second(c, hd, slot):
        tok = slice(c * RET_CHUNK, (c + 1) * RET_CHUNK)
        q = qkr_ref[tok, hd * R_QK_DIM:(hd + 1) * R_QK_DIM]
        k = qkr_ref[tok, R_QK + hd * R_QK_DIM:R_QK + (hd + 1) * R_QK_DIM]
        ft, fl = _fm(c, RET_CHUNK)
        vt = vrt_ref[ft, hd * R_V_DIM:(hd + 1) * R_V_DIM, fl]
        st = states[hd]
        a = (a_scr[slot] * dec_ref[hd]).astype(BF16)
        qx = (q.astype(F32) * xi_ref[hd]).astype(BF16)
        y_scr[slot] = (jnp.dot(vt, a, preferred_element_type=F32)
                       + lax.dot_general(st.astype(BF16), qx, contract_last, preferred_element_type=F32))
        kz = (k.astype(F32) * zeta_ref[hd]).astype(BF16)
        states[hd] = gam_ref[hd] * st + jnp.dot(vt, kz, preferred_element_type=F32)
        return slot

    def ret_third(c, hd, slot):
        y = y_scr[slot]
        vsl = slice(hd * R_V_DIM, (hd + 1) * R_V_DIM)
        mu = jnp.mean(y, axis=0, keepdims=True)
        d = y - mu
        var = jnp.mean(d * d, axis=0, keepdims=True)
        yn = d * lax.rsqrt(var + EPS)
        ft, fl = _fm(c, RET_CHUNK)
        rot_ref[ft, vsl, fl] = (gst_ref[ft, vsl, fl].astype(F32) * yn).astype(BF16)

    att = [(att_first, att_second, att_third, c, kvh)
           for c in range(MIX_TOKENS // BLOCK) for kvh in range(A_KV_HEADS)]
    ret = [(ret_first, ret_second, ret_third, c, hd) for c in range(MIX_TOKENS // RET_CHUNK) for hd in range(R_HEADS)]
    items = [it for pair in zip(att, ret) for it in pair]
    firsts, seconds = {}, {}
    for i in range(len(items) + LAG_A + LAG_B):
        if i < len(items):
            f1, _, _, c, j = items[i]
            firsts[i] = f1(c, j)
        k = i - LAG_A
        if 0 <= k < len(items):
            _, f2, _, c, j = items[k]
            seconds[k] = f2(c, j, firsts.pop(k))
        k = i - LAG_A - LAG_B
        if 0 <= k < len(items):
            _, _, f3, c, j = items[k]
            f3(c, j, seconds.pop(k))
    for hd in range(R_HEADS):
        state_ref[hd] = states[hd]


def _mixer(qa, ka, vt, qkr, vrt, gst, bias, sinks, tables, batch, seq):
    decay_t, zeta_b, xi_b, gamma_c = tables
    t = qa.shape[0]
    tm = MIX_TOKENS
    steps = seq // tm
    nb = tm // BLOCK
    row = lambda n: pl.BlockSpec((tm, n), lambda b, c: (b * steps + c, 0))
    col = lambda n: pl.BlockSpec((tm // FM_TILE, n, FM_TILE), lambda b, c: (b * steps + c, 0, 0))
    prev_block = lambda b, c: b * steps * nb + jnp.maximum(c * nb - 1, 0)
    prow = pl.BlockSpec((BLOCK, A_KV), lambda b, c: (prev_block(b, c), 0))
    per_tile = FM_TILE // BLOCK
    pcol = pl.BlockSpec((None, 2 * LANES, BLOCK),
                        lambda b, c: (prev_block(b, c) // per_tile, 0, prev_block(b, c) % per_tile))
    smem = pl.BlockSpec(memory_space=pltpu.SMEM)
    return pl.pallas_call(
        _mixer_kernel,
        grid=(batch, steps),
        in_specs=[smem, smem, row(A_Q), row(A_KV), prow, col(2 * LANES), pcol, row(2 * R_QK), col(R_V), col(R_V),
                  _resident((2, A_KV_HEADS, 2 * BLOCK, A_GROUP * BLOCK)),
                  _resident((R_HEADS, RET_CHUNK, RET_CHUNK)), _resident((R_HEADS, RET_CHUNK, R_QK_DIM)),
                  _resident((R_HEADS, RET_CHUNK, R_QK_DIM))],
        out_specs=[col(A_Q), col(R_V)],
        scratch_shapes=[pltpu.VMEM((R_HEADS, R_V_DIM, R_QK_DIM), F32),
                        pltpu.VMEM((STAGE_SLOTS, 2 * BLOCK, A_GROUP * BLOCK), F32),
                        pltpu.VMEM((STAGE_SLOTS, LANES, A_GROUP * BLOCK), F32),
                        pltpu.VMEM((STAGE_SLOTS, RET_CHUNK, RET_CHUNK), F32),
                        pltpu.VMEM((STAGE_SLOTS, R_V_DIM, RET_CHUNK), F32)],
        out_shape=[jax.ShapeDtypeStruct((t // FM_TILE, A_Q, FM_TILE), BF16),
                   jax.ShapeDtypeStruct((t // FM_TILE, R_V, FM_TILE), BF16)],
        compiler_params=pltpu.CompilerParams(dimension_semantics=("parallel", "arbitrary"),
                                             vmem_limit_bytes=V7X_VMEM_LIMIT_BYTES),
        name="mixer",
    )(sinks.astype(F32), gamma_c, qa, ka, ka, vt, vt, qkr, vrt, gst, bias, decay_t, zeta_b, xi_b)


def _gelu_tanh(v):
    c = math.sqrt(2.0 / math.pi)
    hv = 0.5 * v
    return hv + hv * jnp.tanh(v * (c + (c * 0.044715) * (v * v)))


def _tail_kernel(tiles_per_seq, x_ref, aot_ref, rot_ref, gates_ref, woa_ref, wor_ref, wo_ref,
                 npm_ref, npf_ref, wup_ref, cw_ref, cb_ref, wdn_ref, nff_ref,
                 out_ref, carry_ref, hid_ref):
    tm = x_ref.shape[0]
    hm = tm // TAIL_SPLIT
    contract_first = (((0,), (0,)), ((), ()))

    @pl.when(pl.program_id(0) % tiles_per_seq == 0)
    def _():
        carry_ref[...] = jnp.zeros_like(carry_ref)

    def mix_out(rs):
        ya = lax.dot_general(aot_ref[:, rs], woa_ref[...], contract_first, preferred_element_type=F32)
        yr = lax.dot_general(rot_ref[:, rs], wor_ref[...], contract_first, preferred_element_type=F32)
        gate_a = gates_ref[rs, :D_MODEL].astype(F32)
        gate_r = gates_ref[rs, D_MODEL:].astype(F32)
        merged = (gate_a * ya + gate_r * yr).astype(BF16)
        x1 = x_ref[rs, :] + _rms(jnp.dot(merged, wo_ref[...], preferred_element_type=F32), npm_ref[...])
        return x1, _rms(x1, npf_ref[...]).astype(BF16)

    def ffn(rs, x1, h):
        row = lax.broadcasted_iota(jnp.int32, (hm, FF_CHUNK), 0)
        for c in range(D_FF // FF_CHUNK):
            cs = slice(c * FF_CHUNK, (c + 1) * FF_CHUNK)
            a = jnp.dot(h, wup_ref[:, cs], preferred_element_type=F32)
            val = jnp.dot(h, wup_ref[:, D_FF + c * FF_CHUNK:D_FF + (c + 1) * FF_CHUNK], preferred_element_type=F32)
            last = carry_ref[SUBLANES - 1:SUBLANES, cs]
            last2 = carry_ref[SUBLANES - 2:SUBLANES - 1, cs]
            a1 = jnp.where(row == 0, last, pltpu.roll(a, 1, 0))
            a2 = jnp.where(row == 0, last2, jnp.where(row == 1, last, pltpu.roll(a, 2, 0)))
            conv = cb_ref[:, cs] + cw_ref[0:1, cs] * a2 + cw_ref[1:2, cs] * a1 + cw_ref[2:3, cs] * a
            carry_ref[:, cs] = a[hm - SUBLANES:, :]
            hid_ref[rs, cs] = (_gelu_tanh(conv) * val).astype(BF16)
        y = jnp.dot(hid_ref[rs, :], wdn_ref[...], preferred_element_type=F32)
        out_ref[rs, :] = x1 + _rms(y, nff_ref[...])

    groups = [slice(i * hm, (i + 1) * hm) for i in range(TAIL_SPLIT)]
    mixed = [mix_out(rs) for rs in groups]
    for rs, (x1, h) in zip(groups, mixed):
        ffn(rs, x1, h)


def _tail(xf, aot, rot, gates, woa, wor, wo, npm, npf, wup, cw, cb, wdn, nff, layer, seq):
    t = xf.shape[0]
    tm = TM_TAIL
    row = lambda n: pl.BlockSpec((tm, n), lambda i: (i, 0))
    col = lambda n: pl.BlockSpec((None, n, FM_TILE), lambda i: (i, 0, 0))
    vec = lambda v: v.reshape(1, -1).astype(F32)
    return pl.pallas_call(
        functools.partial(_tail_kernel, seq // tm),
        grid=(t // tm,),
        in_specs=[row(D_MODEL), col(A_Q), col(R_V), row(2 * D_MODEL),
                  _layer_resident((A_Q, D_MODEL), layer), _layer_resident((R_V, D_MODEL), layer),
                  _layer_resident((D_MODEL, D_MODEL), layer),
                  _resident((1, D_MODEL)), _resident((1, D_MODEL)),
                  _layer_resident((D_MODEL, 2 * D_FF), layer), _resident((CONV_WIDTH, D_FF)), _resident((1, D_FF)),
                  _layer_resident((D_FF, D_MODEL), layer), _resident((1, D_MODEL))],
        out_specs=row(D_MODEL),
        out_shape=jax.ShapeDtypeStruct((t, D_MODEL), F32),
        scratch_shapes=[pltpu.VMEM((SUBLANES, D_FF), F32), pltpu.VMEM((tm, D_FF), BF16)],
        compiler_params=pltpu.CompilerParams(dimension_semantics=("arbitrary",),
                                             vmem_limit_bytes=V7X_VMEM_LIMIT_BYTES),
        name="tail",
    )(xf, aot, rot, gates, woa, wor, wo, vec(npm), vec(npf), wup, cw.astype(F32), vec(cb), wdn, vec(nff))


def kernel(x, norm_pre_mix, w_in, sinks, rel_bias, ret_norm, w_out_a, w_out_r, w_out, norm_post_mix,
           norm_pre_ffn, w_up, conv_w, conv_b, w_down, norm_post_ffn):
    batch, seq, _ = x.shape
    depth = w_in.shape[0]
    assert seq % MIX_TOKENS == 0 and seq % TM_INPROJ == 0 and seq % TM_TAIL == 0
    assert MIX_TOKENS % FM_TILE == 0 and TM_INPROJ % FM_TILE == 0 and FM_TILE % (TM_INPROJ // INPROJ_SPLIT) == 0
    xf = x.reshape(batch * seq, D_MODEL).astype(F32)
    cos, sin = _rotary_tables(seq)
    tables = _retention_tables(RET_CHUNK)
    bias = _bias_table(rel_bias)
    w_in, w_out_a, w_out_r, w_out, w_up, w_down = (
        w.astype(BF16) for w in (w_in, w_out_a, w_out_r, w_out, w_up, w_down))
    for l in range(depth):
        qa, ka, vt, qkr, vrt, gst, gates = _inproj(xf, norm_pre_mix[l].astype(F32), w_in, l,
                                                   ret_norm[l].astype(F32), cos, sin, seq)
        aot, rot = _mixer(qa, ka, vt, qkr, vrt, gst, bias, sinks[l], tables, batch, seq)
        xf = _tail(xf, aot, rot, gates, w_out_a, w_out_r, w_out, norm_post_mix[l], norm_pre_ffn[l],
                   w_up, conv_w[l], conv_b[l], w_down, norm_post_ffn[l], l, seq)
    return xf.reshape(batch, seq, D_MODEL).astype(x.dtype)
```

```python
import functools
import math

import numpy as np
import jax
import jax.numpy as jnp
from jax import lax
from jax.experimental import pallas as pl
from jax.experimental.pallas import tpu as pltpu

D_MODEL = 1024
A_HEADS = 8
A_KV_HEADS = 2
A_HEAD_DIM = 64
A_GROUP = A_HEADS // A_KV_HEADS
WINDOW = 128
BLOCK = 128
NUM_BUCKETS = 32
MAX_DISTANCE = 128
R_HEADS = 4
R_QK_DIM = 128
R_V_DIM = 256
ROPE_BASE = 10000.0
D_FF = 2816
CONV_WIDTH = 3
EPS = 1e-6
NEG_INF = -1e30
LOG2E = math.log2(math.e)

A_Q = A_HEADS * A_HEAD_DIM
A_KV = A_KV_HEADS * A_HEAD_DIM
R_QK = R_HEADS * R_QK_DIM
R_V = R_HEADS * R_V_DIM
OFF_AK = A_Q
OFF_AV = OFF_AK + A_KV
OFF_RQK = OFF_AV + A_KV
OFF_RV = OFF_RQK + 2 * R_QK
OFF_RG = OFF_RV + R_V
OFF_GATES = OFF_RG + R_V
D_IN = OFF_GATES + 2 * D_MODEL
V7X_VMEM_LIMIT_BYTES = 56 * 1024 * 1024
SUBLANES = 8
LANES = 128
FF_CHUNK = 256
TM_INPROJ = 1024
INPROJ_SPLIT = 4
TM_TAIL = 512
FM_TILE = TM_TAIL
TAIL_SPLIT = 2
MIX_TOKENS = 8 * BLOCK
LAG_A = 2
LAG_B = 1
STAGE_SLOTS = 4
RET_CHUNK = 2 * BLOCK

TOK_AK = A_Q
TOK_RQK = TOK_AK + A_KV
TOK_WIDTH = TOK_RQK + 2 * R_QK
FEAT_RV = 2 * LANES
FEAT_GATE = FEAT_RV + R_V
FEAT_ROWS = FEAT_GATE + R_V
MIX_ROWS = A_Q + R_V

BF16 = jnp.bfloat16
F32 = jnp.float32


def _resident(shape):
    nd = len(shape)
    return pl.BlockSpec(shape, lambda *_: (0,) * nd, pipeline_mode=pl.Buffered(1))


def _layer_resident(shape, layer):
    nd = len(shape)
    return pl.BlockSpec((None,) + tuple(shape), lambda *_: (layer,) + (0,) * nd, pipeline_mode=pl.Buffered(1))


def _rms(v, gain):
    return v * lax.rsqrt(jnp.mean(v * v, axis=-1, keepdims=True) + EPS) * gain


def _t5_band_buckets():
    i = np.arange(BLOCK)[:, None]
    j = np.arange(2 * BLOCK)[None, :]
    dist = BLOCK + i - j
    n = np.maximum(dist, 0)
    max_exact = NUM_BUCKETS // 2
    large = max_exact + (np.log(np.maximum(n, 1) / max_exact) / math.log(MAX_DISTANCE / max_exact)
                         * (NUM_BUCKETS - max_exact)).astype(np.int32)
    large = np.minimum(large, NUM_BUCKETS - 1)
    bucket = np.where(n < max_exact, n, large).astype(np.int32)
    valid = (dist >= 0) & (dist < WINDOW)
    return bucket, valid


def _rotary_tables(seq):
    half = R_QK_DIM // 2
    freqs = ROPE_BASE ** (-jnp.arange(half, dtype=F32) / half)
    ang = jnp.arange(seq).astype(F32)[:, None] * freqs[None, :]
    cos, sin = jnp.cos(ang), jnp.sin(ang)
    return jnp.concatenate([cos, cos], axis=-1), jnp.concatenate([-sin, sin], axis=-1)


def _retention_tables(chunk):
    log_gamma = jnp.log(1.0 - 2.0 ** (-5.0 - jnp.arange(R_HEADS, dtype=F32)))
    idx = jnp.arange(chunk, dtype=F32)
    rel = idx[None, :] - idx[:, None]
    decay_t = jnp.where(rel >= 0, jnp.exp(log_gamma[:, None, None] * jnp.maximum(rel, 0.0)), 0.0)
    zeta = jnp.exp(log_gamma[:, None] * (chunk - 1 - idx)[None, :])
    xi = jnp.exp(log_gamma[:, None] * (idx + 1.0)[None, :])
    gamma_c = jnp.exp(log_gamma * chunk)
    zeta_b = jnp.broadcast_to(zeta[:, :, None], (R_HEADS, chunk, R_QK_DIM))
    xi_b = jnp.broadcast_to(xi[:, :, None], (R_HEADS, chunk, R_QK_DIM))
    return decay_t, zeta_b, xi_b, gamma_c


def _bias_kernel(tab_ref, bucket_ref, valid_ref, out_ref):
    bucket = bucket_ref[...]
    valid = valid_ref[...] > 0
    in_cur = lax.broadcasted_iota(jnp.int32, (2 * BLOCK, BLOCK), 0) >= BLOCK
    for h in range(A_HEADS):
        kvh, g = divmod(h, A_GROUP)
        acc = jnp.zeros((2 * BLOCK, BLOCK), F32)
        for b in range(NUM_BUCKETS):
            acc = jnp.where(bucket == b, tab_ref[b, h] * LOG2E, acc)
        out_ref[1, kvh, :, g * BLOCK:(g + 1) * BLOCK] = jnp.where(valid, acc, NEG_INF)
        out_ref[0, kvh, :, g * BLOCK:(g + 1) * BLOCK] = jnp.where(valid & in_cur, acc, NEG_INF)


def _bias_table(rel_bias):
    bucket, valid = _t5_band_buckets()
    return pl.pallas_call(
        _bias_kernel,
        out_shape=jax.ShapeDtypeStruct((2, A_KV_HEADS, 2 * BLOCK, A_GROUP * BLOCK), F32),
        in_specs=[pl.BlockSpec(memory_space=pltpu.SMEM),
                  pl.BlockSpec(memory_space=pltpu.VMEM),
                  pl.BlockSpec(memory_space=pltpu.VMEM)],
        out_specs=pl.BlockSpec(memory_space=pltpu.VMEM),
        name="bias_table",
    )(rel_bias.astype(F32), jnp.asarray(bucket.T), jnp.asarray(valid.T.astype(np.int32)))


def _inproj_kernel(x_ref, g_ref, w_ref, rn_ref, cos_ref, sin_ref,
                   tok_ref, feat_ref, gates_ref):
    tm = x_ref.shape[0]
    hm = tm // INPROJ_SPLIT
    groups = [slice(i * hm, (i + 1) * hm) for i in range(INPROJ_SPLIT)]
    hs = [_rms(x_ref[rs, :], g_ref[...]).astype(BF16) for rs in groups]
    ones = jnp.ones((A_HEAD_DIM, hm), BF16)

    for rs, h in zip(groups, hs):
        ft = rs.start // FM_TILE
        fl = slice(rs.start % FM_TILE, rs.start % FM_TILE + hm)

        def proj(lo, hi):
            return jnp.dot(h, w_ref[:, lo:hi], preferred_element_type=F32)

        tok_ref[rs, :TOK_AK] = (proj(0, OFF_AK) * (A_HEAD_DIM ** -0.5 * LOG2E)).astype(BF16)
        tok_ref[rs, TOK_AK:TOK_RQK] = proj(OFF_AK, OFF_AV).astype(BF16)
        vat = proj(OFF_AV, OFF_RQK).astype(BF16).T
        for kvh in range(A_KV_HEADS):
            feat_ref[ft, kvh * LANES:kvh * LANES + A_HEAD_DIM, fl] = vat[kvh * A_HEAD_DIM:(kvh + 1) * A_HEAD_DIM, :]
            feat_ref[ft, kvh * LANES + A_HEAD_DIM:(kvh + 1) * LANES, fl] = ones
        feat_ref[ft, FEAT_RV:FEAT_GATE, fl] = proj(OFF_RV, OFF_RG).astype(BF16).T
        g = proj(OFF_RG, OFF_GATES)
        feat_ref[ft, FEAT_GATE:, fl] = (g * jax.nn.sigmoid(g) * rn_ref[...]).astype(BF16).T
        qk = proj(OFF_RQK, OFF_RV)
        cos = cos_ref[rs, :]
        sin = sin_ref[rs, :]
        for j in range(2 * R_HEADS):
            xh = qk[:, j * R_QK_DIM:(j + 1) * R_QK_DIM]
            rot = xh * cos + pltpu.roll(xh, R_QK_DIM // 2, 1) * sin
            if j >= R_HEADS:
                rot = rot * (R_QK_DIM ** -0.5)
            tok_ref[rs, TOK_RQK + j * R_QK_DIM:TOK_RQK + (j + 1) * R_QK_DIM] = rot.astype(BF16)
        gates_ref[rs, :] = jax.nn.sigmoid(proj(OFF_GATES, D_IN)).astype(BF16)


def _inproj(xf, gain, w, layer, ret_norm, cos, sin, seq):
    t = xf.shape[0]
    tm = TM_INPROJ
    tiles_per_seq = seq // tm
    row = lambda n: pl.BlockSpec((tm, n), lambda i: (i, 0))
    col = lambda n: pl.BlockSpec((tm // FM_TILE, n, FM_TILE), lambda i: (i, 0, 0))
    pos = pl.BlockSpec((tm, R_QK_DIM), lambda i: (i % tiles_per_seq, 0))
    tok = lambda n: jax.ShapeDtypeStruct((t, n), BF16)
    feat = lambda n: jax.ShapeDtypeStruct((t // FM_TILE, n, FM_TILE), BF16)
    return pl.pallas_call(
        _inproj_kernel,
        grid=(t // tm,),
        in_specs=[row(D_MODEL), _resident((1, D_MODEL)), _layer_resident((D_MODEL, D_IN), layer),
                  _resident((1, R_V)), pos, pos],
        out_specs=[row(TOK_WIDTH), col(FEAT_ROWS), row(2 * D_MODEL)],
        out_shape=[tok(TOK_WIDTH), feat(FEAT_ROWS), tok(2 * D_MODEL)],
        compiler_params=pltpu.CompilerParams(dimension_semantics=("parallel",),
                                             vmem_limit_bytes=V7X_VMEM_LIMIT_BYTES),
        name="inproj",
    )(xf, gain.reshape(1, D_MODEL), w, ret_norm.reshape(1, R_V), cos, sin)


def _fm(i, width):
    start = i * width
    return start // FM_TILE, slice(start % FM_TILE, start % FM_TILE + width)


def _mixer_kernel(sink_ref, gam_ref, tok_ref, kp_ref, feat_ref, vtp_ref,
                  bias_ref, dec_ref, zeta_ref, xi_ref, out_ref, state_ref, s_scr, o_scr, a_scr, y_scr):
    step = pl.program_id(1)

    @pl.when(step == 0)
    def _():
        state_ref[...] = jnp.zeros_like(state_ref)

    contract_last = (((1,), (1,)), ((), ()))

    lane_head = lax.broadcasted_iota(jnp.int32, (1, A_GROUP * BLOCK), 1) // BLOCK
    sink_rows = []
    for kvh in range(A_KV_HEADS):
        r = jnp.full((1, A_GROUP * BLOCK), sink_ref[kvh * A_GROUP] * LOG2E, F32)
        for g in range(1, A_GROUP):
            r = jnp.where(lane_head == g, sink_ref[kvh * A_GROUP + g] * LOG2E, r)
        sink_rows.append(r)

    def att_first(c, kvh):
        cur = slice(c * BLOCK, (c + 1) * BLOCK)
        prv = slice((c - 1) * BLOCK, c * BLOCK)
        ks = slice(kvh * A_HEAD_DIM, (kvh + 1) * A_HEAD_DIM)
        kcols = slice(TOK_AK + ks.start, TOK_AK + ks.stop)
        k_prev = kp_ref[:, ks] if c == 0 else tok_ref[prv, kcols]
        k_band = jnp.concatenate([k_prev, tok_ref[cur, kcols]], axis=0)
        q = jnp.concatenate([tok_ref[cur, (kvh * A_GROUP + g) * A_HEAD_DIM:(kvh * A_GROUP + g + 1) * A_HEAD_DIM]
                             for g in range(A_GROUP)], axis=0)
        slot = (c * A_KV_HEADS + kvh) % STAGE_SLOTS
        s_scr[slot] = lax.dot_general(k_band, q, contract_last, preferred_element_type=F32)
        return slot

    def att_second(c, kvh, slot):
        vs = slice(kvh * LANES, (kvh + 1) * LANES)
        variant = jnp.minimum(step, 1) if c == 0 else 1
        ft, fl = _fm(c, BLOCK)
        pt, pls = _fm(c - 1, BLOCK)
        v_prev = vtp_ref[vs, :] if c == 0 else feat_ref[pt, vs, pls]
        v_band = jnp.concatenate([v_prev, feat_ref[ft, vs, fl]], axis=1)
        s = s_scr[slot] + bias_ref[variant, kvh]
        sink = sink_rows[kvh]
        m = jnp.maximum(jnp.max(s, axis=0, keepdims=True), sink)
        p = jnp.exp2(s - m).astype(BF16)
        o_scr[slot] = jnp.dot(v_band, p, preferred_element_type=F32)
        return slot, jnp.exp2(sink - m)

    def att_third(c, kvh, second):
        slot, esink = second
        o = o_scr[slot]
        ft, fl = _fm(c, BLOCK)
        inv = 1.0 / (o[A_HEAD_DIM:A_HEAD_DIM + 1, :] + esink)
        res = (o[:A_HEAD_DIM, :] * inv).astype(BF16)
        for g in range(A_GROUP):
            hd = kvh * A_GROUP + g
            out_ref[ft, hd * A_HEAD_DIM:(hd + 1) * A_HEAD_DIM, fl] = res[:, g * BLOCK:(g + 1) * BLOCK]

    states = [state_ref[hd] for hd in range(R_HEADS)]

    def ret_first(c, hd):
        tok = slice(c * RET_CHUNK, (c + 1) * RET_CHUNK)
        q = tok_ref[tok, TOK_RQK + hd * R_QK_DIM:TOK_RQK + (hd + 1) * R_QK_DIM]
        k = tok_ref[tok, TOK_RQK + R_QK + hd * R_QK_DIM:TOK_RQK + R_QK + (hd + 1) * R_QK_DIM]
        slot = (c * R_HEADS + hd) % STAGE_SLOTS
        a_scr[slot] = lax.dot_general(k, q, contract_last, preferred_element_type=F32)
        return slot

    def ret_second(c, hd, slot):
        tok = slice(c * RET_CHUNK, (c + 1) * RET_CHUNK)
        q = tok_ref[tok, TOK_RQK + hd * R_QK_DIM:TOK_RQK + (hd + 1) * R_QK_DIM]
        k = tok_ref[tok, TOK_RQK + R_QK + hd * R_QK_DIM:TOK_RQK + R_QK + (hd + 1) * R_QK_DIM]
        ft, fl = _fm(c, RET_CHUNK)
        vt = feat_ref[ft, FEAT_RV + hd * R_V_DIM:FEAT_RV + (hd + 1) * R_V_DIM, fl]
        st = states[hd]
        a = (a_scr[slot] * dec_ref[hd]).astype(BF16)
        qx = (q.astype(F32) * xi_ref[hd]).astype(BF16)
        y_scr[slot] = (jnp.dot(vt, a, preferred_element_type=F32)
                       + lax.dot_general(st.astype(BF16), qx, contract_last, preferred_element_type=F32))
        kz = (k.astype(F32) * zeta_ref[hd]).astype(BF16)
        states[hd] = gam_ref[hd] * st + jnp.dot(vt, kz, preferred_element_type=F32)
        return slot

    def ret_third(c, hd, slot):
        y = y_scr[slot]
        lo, hi = hd * R_V_DIM, (hd + 1) * R_V_DIM
        mu = jnp.mean(y, axis=0, keepdims=True)
        d = y - mu
        var = jnp.mean(d * d, axis=0, keepdims=True)
        yn = d * lax.rsqrt(var + EPS)
        ft, fl = _fm(c, RET_CHUNK)
        gate = feat_ref[ft, FEAT_GATE + lo:FEAT_GATE + hi, fl].astype(F32)
        out_ref[ft, A_Q + lo:A_Q + hi, fl] = (gate * yn).astype(BF16)

    att = [(att_first, att_second, att_third, c, kvh)
           for c in range(MIX_TOKENS // BLOCK) for kvh in range(A_KV_HEADS)]
    ret = [(ret_first, ret_second, ret_third, c, hd) for c in range(MIX_TOKENS // RET_CHUNK) for hd in range(R_HEADS)]
    items = [it for pair in zip(att, ret) for it in pair]
    firsts, seconds = {}, {}
    for i in range(len(items) + LAG_A + LAG_B):
        if i < len(items):
            f1, _, _, c, j = items[i]
            firsts[i] = f1(c, j)
        k = i - LAG_A
        if 0 <= k < len(items):
            _, f2, _, c, j = items[k]
            seconds[k] = f2(c, j, firsts.pop(k))
        k = i - LAG_A - LAG_B
        if 0 <= k < len(items):
            _, _, f3, c, j = items[k]
            f3(c, j, seconds.pop(k))
    for hd in range(R_HEADS):
        state_ref[hd] = states[hd]


def _mixer(tokm, featm, bias, sinks, tables, batch, seq):
    decay_t, zeta_b, xi_b, gamma_c = tables
    t = tokm.shape[0]
    tm = MIX_TOKENS
    steps = seq // tm
    nb = tm // BLOCK
    row = lambda n: pl.BlockSpec((tm, n), lambda b, c: (b * steps + c, 0))
    col = lambda n: pl.BlockSpec((tm // FM_TILE, n, FM_TILE), lambda b, c: (b * steps + c, 0, 0))
    prev_block = lambda b, c: b * steps * nb + jnp.maximum(c * nb - 1, 0)
    prow = pl.BlockSpec((BLOCK, A_KV), lambda b, c: (prev_block(b, c), TOK_AK // A_KV))
    per_tile = FM_TILE // BLOCK
    pcol = pl.BlockSpec((None, 2 * LANES, BLOCK),
                        lambda b, c: (prev_block(b, c) // per_tile, 0, prev_block(b, c) % per_tile))
    smem = pl.BlockSpec(memory_space=pltpu.SMEM)
    return pl.pallas_call(
        _mixer_kernel,
        grid=(batch, steps),
        in_specs=[smem, smem, row(TOK_WIDTH), prow, col(FEAT_ROWS), pcol,
                  _resident((2, A_KV_HEADS, 2 * BLOCK, A_GROUP * BLOCK)),
                  _resident((R_HEADS, RET_CHUNK, RET_CHUNK)), _resident((R_HEADS, RET_CHUNK, R_QK_DIM)),
                  _resident((R_HEADS, RET_CHUNK, R_QK_DIM))],
        out_specs=col(MIX_ROWS),
        scratch_shapes=[pltpu.VMEM((R_HEADS, R_V_DIM, R_QK_DIM), F32),
                        pltpu.VMEM((STAGE_SLOTS, 2 * BLOCK, A_GROUP * BLOCK), F32),
                        pltpu.VMEM((STAGE_SLOTS, LANES, A_GROUP * BLOCK), F32),
                        pltpu.VMEM((STAGE_SLOTS, RET_CHUNK, RET_CHUNK), F32),
                        pltpu.VMEM((STAGE_SLOTS, R_V_DIM, RET_CHUNK), F32)],
        out_shape=jax.ShapeDtypeStruct((t // FM_TILE, MIX_ROWS, FM_TILE), BF16),
        compiler_params=pltpu.CompilerParams(dimension_semantics=("parallel", "arbitrary"),
                                             vmem_limit_bytes=V7X_VMEM_LIMIT_BYTES),
        name="mixer",
    )(sinks.astype(F32), gamma_c, tokm, tokm, featm, featm, bias, decay_t, zeta_b, xi_b)


def _gelu_tanh(v):
    c = math.sqrt(2.0 / math.pi)
    hv = 0.5 * v
    return hv + hv * jnp.tanh(v * (c + (c * 0.044715) * (v * v)))


def _tail_kernel(tiles_per_seq, x_ref, mix_ref, gates_ref, woa_ref, wor_ref, wo_ref,
                 npm_ref, npf_ref, wup_ref, cw_ref, cb_ref, wdn_ref, nff_ref,
                 out_ref, carry_ref, hid_ref):
    tm = x_ref.shape[0]
    hm = tm // TAIL_SPLIT
    contract_first = (((0,), (0,)), ((), ()))

    @pl.when(pl.program_id(0) % tiles_per_seq == 0)
    def _():
        carry_ref[...] = jnp.zeros_like(carry_ref)

    def mix_out(rs):
        ya = lax.dot_general(mix_ref[:A_Q, rs], woa_ref[...], contract_first, preferred_element_type=F32)
        yr = lax.dot_general(mix_ref[A_Q:, rs], wor_ref[...], contract_first, preferred_element_type=F32)
        gate_a = gates_ref[rs, :D_MODEL].astype(F32)
        gate_r = gates_ref[rs, D_MODEL:].astype(F32)
        merged = (gate_a * ya + gate_r * yr).astype(BF16)
        x1 = x_ref[rs, :] + _rms(jnp.dot(merged, wo_ref[...], preferred_element_type=F32), npm_ref[...])
        return x1, _rms(x1, npf_ref[...]).astype(BF16)

    def ffn(rs, x1, h):
        row = lax.broadcasted_iota(jnp.int32, (hm, FF_CHUNK), 0)
        for c in range(D_FF // FF_CHUNK):
            cs = slice(c * FF_CHUNK, (c + 1) * FF_CHUNK)
            a = jnp.dot(h, wup_ref[:, cs], preferred_element_type=F32)
            val = jnp.dot(h, wup_ref[:, D_FF + c * FF_CHUNK:D_FF + (c + 1) * FF_CHUNK], preferred_element_type=F32)
            last = carry_ref[SUBLANES - 1:SUBLANES, cs]
            last2 = carry_ref[SUBLANES - 2:SUBLANES - 1, cs]
            a1 = jnp.where(row == 0, last, pltpu.roll(a, 1, 0))
            a2 = jnp.where(row == 0, last2, jnp.where(row == 1, last, pltpu.roll(a, 2, 0)))
            conv = cb_ref[:, cs] + cw_ref[0:1, cs] * a2 + cw_ref[1:2, cs] * a1 + cw_ref[2:3, cs] * a
            carry_ref[:, cs] = a[hm - SUBLANES:, :]
            hid_ref[rs, cs] = (_gelu_tanh(conv) * val).astype(BF16)
        y = jnp.dot(hid_ref[rs, :], wdn_ref[...], preferred_element_type=F32)
        out_ref[rs, :] = x1 + _rms(y, nff_ref[...])

    groups = [slice(i * hm, (i + 1) * hm) for i in range(TAIL_SPLIT)]
    mixed = [mix_out(rs) for rs in groups]
    for rs, (x1, h) in zip(groups, mixed):
        ffn(rs, x1, h)


def _tail(xf, mixed, gates, woa, wor, wo, npm, npf, wup, cw, cb, wdn, nff, layer, seq):
    t = xf.shape[0]
    tm = TM_TAIL
    row = lambda n: pl.BlockSpec((tm, n), lambda i: (i, 0))
    col = lambda n: pl.BlockSpec((None, n, FM_TILE), lambda i: (i, 0, 0))
    vec = lambda v: v.reshape(1, -1).astype(F32)
    return pl.pallas_call(
        functools.partial(_tail_kernel, seq // tm),
        grid=(t // tm,),
        in_specs=[row(D_MODEL), col(MIX_ROWS), row(2 * D_MODEL),
                  _layer_resident((A_Q, D_MODEL), layer), _layer_resident((R_V, D_MODEL), layer),
                  _layer_resident((D_MODEL, D_MODEL), layer),
                  _resident((1, D_MODEL)), _resident((1, D_MODEL)),
                  _layer_resident((D_MODEL, 2 * D_FF), layer), _resident((CONV_WIDTH, D_FF)), _resident((1, D_FF)),
                  _layer_resident((D_FF, D_MODEL), layer), _resident((1, D_MODEL))],
        out_specs=row(D_MODEL),
        out_shape=jax.ShapeDtypeStruct((t, D_MODEL), F32),
        scratch_shapes=[pltpu.VMEM((SUBLANES, D_FF), F32), pltpu.VMEM((tm, D_FF), BF16)],
        compiler_params=pltpu.CompilerParams(dimension_semantics=("arbitrary",),
                                             vmem_limit_bytes=V7X_VMEM_LIMIT_BYTES),
        name="tail",
    )(xf, mixed, gates, woa, wor, wo, vec(npm), vec(npf), wup, cw.astype(F32), vec(cb), wdn, vec(nff))


def kernel(x, norm_pre_mix, w_in, sinks, rel_bias, ret_norm, w_out_a, w_out_r, w_out, norm_post_mix,
           norm_pre_ffn, w_up, conv_w, conv_b, w_down, norm_post_ffn):
    batch, seq, _ = x.shape
    depth = w_in.shape[0]
    assert seq % MIX_TOKENS == 0 and seq % TM_INPROJ == 0 and seq % TM_TAIL == 0
    assert MIX_TOKENS % FM_TILE == 0 and TM_INPROJ % FM_TILE == 0 and FM_TILE % (TM_INPROJ // INPROJ_SPLIT) == 0
    xf = x.reshape(batch * seq, D_MODEL).astype(F32)
    cos, sin = _rotary_tables(seq)
    tables = _retention_tables(RET_CHUNK)
    bias = _bias_table(rel_bias)
    w_in, w_out_a, w_out_r, w_out, w_up, w_down = (
        w.astype(BF16) for w in (w_in, w_out_a, w_out_r, w_out, w_up, w_down))
    for l in range(depth):
        tokm, featm, gates = _inproj(xf, norm_pre_mix[l].astype(F32), w_in, l,
                                     ret_norm[l].astype(F32), cos, sin, seq)
        mixed = _mixer(tokm, featm, bias, sinks[l], tables, batch, seq)
        xf = _tail(xf, mixed, gates, w_out_a, w_out_r, w_out, norm_post_mix[l], norm_pre_ffn[l],
                   w_up, conv_w[l], conv_b[l], w_down, norm_post_ffn[l], l, seq)
    return xf.reshape(batch, seq, D_MODEL).astype(x.dtype)
```
